```python
import jax, jax.numpy as jnp
from jax import lax
import numpy as np

D_MODEL = 2048
BATCH = 8
SEQ = 4096
DEPTH = 2
DEC_BATCH = 16
DEC_SEQ = 64
PAST_LEN = 2048

CHUNK = 64
C_CONV = D_MODEL // 2
C_RET = D_MODEL // 2
D_MIX = C_CONV + C_RET
N_RET_HEADS = 8
RET_HEAD_DIM = C_RET // N_RET_HEADS
CONV_WIDTH = 31
CONV_STATE = CONV_WIDTH - 1
D_IN = 3 * C_CONV + 4 * C_RET
ROPE_BASE = 10000.0
EPS = 1e-6

kernel_name = "hybrid_conformer_conv_retention_stream_step"


def rms_norm(x, w):
    xf = x.astype(jnp.float32)
    y = xf * lax.rsqrt(jnp.mean(xf * xf, axis=-1, keepdims=True) + EPS)
    return (y * w.astype(jnp.float32)).astype(x.dtype)


def layer_norm(x, w, b):
    xf = x.astype(jnp.float32)
    mu = jnp.mean(xf, axis=-1, keepdims=True)
    xc = xf - mu
    var = jnp.mean(xc * xc, axis=-1, keepdims=True)
    y = xc * lax.rsqrt(var + EPS) * w.astype(jnp.float32) + b.astype(jnp.float32)
    return y.astype(x.dtype)


def rotary(x, pos):
    half = x.shape[-1] // 2
    inv_freq = ROPE_BASE ** (-jnp.arange(half, dtype=jnp.float32) / half)
    ang = pos.astype(jnp.float32)[:, None] * inv_freq[None, :]
    cos = jnp.cos(ang)[None, :, None, :]
    sin = jnp.sin(ang)[None, :, None, :]
    x1, x2 = x[..., :half], x[..., half:]
    return jnp.concatenate([x1 * cos - x2 * sin, x1 * sin + x2 * cos], axis=-1)


def retention_log_decay():
    return jnp.log1p(-jnp.exp2(-5.0 - jnp.arange(N_RET_HEADS, dtype=jnp.float32)))


def retention_block(R, qkv, log_g):
    q, k, v = qkv
    L = q.shape[2]
    i = jnp.arange(L, dtype=jnp.float32)
    lg = log_g[:, None]
    dmat = jnp.exp(jnp.abs(i[:, None] - i[None, :])[None] * log_g[:, None, None])
    s = jnp.einsum('bhid,bhjd->bhij', q, k) * dmat[None]
    intra = jnp.einsum('bhij,bhje->bhie', s, v)
    q_dec = q * jnp.exp((i + 1.0)[None, :] * lg)[None, :, :, None]
    inter = jnp.einsum('bhid,bhde->bhie', q_dec, R)
    k_dec = k * jnp.exp((L - 1.0 - i)[None, :] * lg)[None, :, :, None]
    R_new = R * jnp.exp(L * log_g)[None, :, None, None] + jnp.einsum('bhjd,bhje->bhde', k_dec, v)
    return R_new, intra + inter


def retention(q, k, v, R0, log_g):
    B, T, H, d = q.shape
    L = min(T, CHUNK)
    n = T // L

    def to_blocks(a):
        return a.reshape(B, n, L, H, d).transpose(1, 0, 3, 2, 4)

    R, o = lax.scan(lambda R, xs: retention_block(R, xs, log_g), R0,
                    (to_blocks(q), to_blocks(k), to_blocks(v)))
    o = o.transpose(1, 0, 3, 2, 4).reshape(B, T, H, d)
    return o, R


def causal_dwconv(u, buf, w, b):
    full = jnp.concatenate([buf.astype(u.dtype), u], axis=1)
    y = lax.conv_general_dilated(full, w[:, None, :].astype(u.dtype), window_strides=(1,),
                                 padding='VALID', dimension_numbers=('NWC', 'WIO', 'NWC'),
                                 feature_group_count=u.shape[-1])
    return y + b.astype(u.dtype), full[:, -CONV_STATE:]


def mixer(h, pos, conv_buf, R0, w_in, conv_w, conv_b, ln_w, ln_b, w_pw, gn_w, w_out, log_g):
    B, T, _ = h.shape
    z = h @ w_in
    a, a_gate, g_conv, q, k, v, g_ret = jnp.split(
        z, [C_CONV, 2 * C_CONV, 3 * C_CONV, 3 * C_CONV + C_RET,
            3 * C_CONV + 2 * C_RET, 3 * C_CONV + 3 * C_RET], axis=-1)
    u = a * jax.nn.sigmoid(a_gate)
    c, new_buf = causal_dwconv(u, conv_buf, conv_w, conv_b)
    c = jax.nn.silu(layer_norm(c, ln_w, ln_b)) @ w_pw
    y_conv = jax.nn.silu(g_conv) * c
    def heads(t):
        return t.reshape(B, T, N_RET_HEADS, RET_HEAD_DIM).astype(jnp.float32)
    qh = rotary(heads(q), pos)
    kh = rotary(heads(k), pos) * (RET_HEAD_DIM ** -0.5)
    vh = heads(v)
    o, R = retention(qh, kh, vh, R0, log_g)
    mu = jnp.mean(o, axis=-1, keepdims=True)
    oc = o - mu
    o = oc * lax.rsqrt(jnp.mean(oc * oc, axis=-1, keepdims=True) + EPS)
    o = o.reshape(B, T, C_RET) * gn_w.astype(jnp.float32)
    y_ret = jax.nn.silu(g_ret) * o.astype(h.dtype)
    y = jnp.concatenate([y_conv, y_ret], axis=-1) @ w_out
    return y, new_buf, R


def trunk(x, pos, conv_bufs, R0s, norm_pre, w_in, conv_w, conv_b, conv_ln_w, conv_ln_b,
          w_pw, ret_gn_w, w_out, norm_post):
    log_g = retention_log_decay()
    bufs, states = [], []
    for l in range(DEPTH):
        h = rms_norm(x, norm_pre[l])
        y, buf, R = mixer(h, pos, conv_bufs[l], R0s[l].astype(jnp.float32), w_in[l], conv_w[l],
                          conv_b[l], conv_ln_w[l], conv_ln_b[l], w_pw[l], ret_gn_w[l],
                          w_out[l], log_g)
        x = x + rms_norm(y, norm_post[l])
        bufs.append(buf)
        states.append(R)
    return x, jnp.stack(bufs), jnp.stack(states)


def setup_inputs(seed: int = 0) -> dict:
    key = jax.random.key(seed)
    ks = jax.random.split(key, 14)
    f32 = jnp.float32
    nrm = lambda k, s: jax.random.normal(k, s, dtype=f32)
    return {
        "x_prompt": nrm(ks[0], (BATCH, SEQ, D_MODEL)),
        "x_sample": nrm(ks[1], (DEC_BATCH, DEC_SEQ, D_MODEL)),
        "cache_conv": nrm(ks[2], (DEPTH, DEC_BATCH, CONV_STATE, C_CONV)) * 0.5,
        "state_ret": nrm(ks[3], (DEPTH, DEC_BATCH, N_RET_HEADS, RET_HEAD_DIM, RET_HEAD_DIM)),
        "norm_pre": 1.0 + 0.01 * nrm(ks[4], (DEPTH, D_MODEL)),
        "w_in": nrm(ks[5], (DEPTH, D_MODEL, D_IN)) * D_MODEL ** -0.5,
        "conv_w": nrm(ks[6], (DEPTH, CONV_WIDTH, C_CONV)) * CONV_WIDTH ** -0.5,
        "conv_b": 0.01 * nrm(ks[7], (DEPTH, C_CONV)),
        "conv_ln_w": 1.0 + 0.01 * nrm(ks[8], (DEPTH, C_CONV)),
        "conv_ln_b": 0.01 * nrm(ks[9], (DEPTH, C_CONV)),
        "w_pw": nrm(ks[10], (DEPTH, C_CONV, C_CONV)) * C_CONV ** -0.5,
        "ret_gn_w": 1.0 + 0.01 * nrm(ks[11], (DEPTH, C_RET)),
        "w_out": nrm(ks[12], (DEPTH, D_MIX, D_MODEL)) * D_MIX ** -0.5,
        "norm_post": 1.0 + 0.01 * nrm(ks[13], (DEPTH, D_MODEL)),
    }


def reference(x_prompt, x_sample, cache_conv, state_ret, norm_pre, w_in, conv_w, conv_b,
              conv_ln_w, conv_ln_b, w_pw, ret_gn_w, w_out, norm_post):
    B, T, _ = x_prompt.shape
    zero_bufs = jnp.zeros((DEPTH, B, CONV_STATE, C_CONV), dtype=x_prompt.dtype)
    zero_R = jnp.zeros((DEPTH, B, N_RET_HEADS, RET_HEAD_DIM, RET_HEAD_DIM), dtype=jnp.float32)
    pos_prompt = jnp.arange(T)
    y_prompt, new_conv_prompt, new_ret_prompt = trunk(
        x_prompt, pos_prompt, zero_bufs, zero_R, norm_pre, w_in, conv_w, conv_b, conv_ln_w,
        conv_ln_b, w_pw, ret_gn_w, w_out, norm_post)
    pos_sample = PAST_LEN + jnp.arange(x_sample.shape[1])
    y_sample, new_conv_sample, new_ret_sample = trunk(
        x_sample, pos_sample, cache_conv, state_ret, norm_pre, w_in, conv_w, conv_b, conv_ln_w,
        conv_ln_b, w_pw, ret_gn_w, w_out, norm_post)
    return (y_prompt, y_sample, new_conv_prompt, new_ret_prompt, new_conv_sample, new_ret_sample)
```

```python
import functools

import jax
import jax.numpy as jnp
from jax import lax
from jax.experimental import pallas as pl
from jax.experimental.pallas import tpu as pltpu

CHUNK = 64
CONV_WIDTH = 31
CONV_STATE = CONV_WIDTH - 1
N_RET_HEADS = 8
ROPE_BASE = 10000.0
EPS = 1e-6

LANES = 128
SUBLANES = 8
HIST_ROWS = 32
HIST_PAD = HIST_ROWS - CONV_STATE
CONV_ROW_STRIDE = 4
PAST_LEN = 2048
VMEM_LIMIT_BYTES = 56 * 1024 * 1024

F32 = jnp.float32
BF16 = jnp.bfloat16


def _rms_norm(xf, w):
    y = xf * lax.rsqrt(jnp.mean(xf * xf, axis=-1, keepdims=True) + EPS)
    return y * w


def _silu(x):
    return x * jax.nn.sigmoid(x)


def _resident(shape):
    nd = len(shape)
    return pl.BlockSpec(shape, lambda *_: (0,) * nd, pipeline_mode=pl.Buffered(1))


def _conv_kernel(x_ref, npre_ref, win_ref, cw_ref, cb_ref, lnw_ref, lnb_ref, wpw_ref, buf0_ref,
                 y_ref, nbuf_ref, ubuf_ref, c_ref, act_ref, *, ns, ts):
    d_model = x_ref.shape[-1]
    c_conv = wpw_ref.shape[0]
    n_slab = c_conv // LANES
    rows = ns * ts

    @pl.when(pl.program_id(1) == 0)
    def _():
        for s in range(ns):
            for cc in range(n_slab):
                ubuf_ref[s, cc, 0:HIST_ROWS, :] = buf0_ref[s, :, cc * LANES:(cc + 1) * LANES]

    x = x_ref[...].reshape(rows, d_model)
    h = _rms_norm(x, npre_ref[...]).astype(BF16)
    a = jnp.dot(h, win_ref[:, 0:c_conv], preferred_element_type=F32)
    b = jnp.dot(h, win_ref[:, c_conv:2 * c_conv], preferred_element_type=F32)
    u = a * jax.nn.sigmoid(b)
    for s in range(ns):
        for cc in range(n_slab):
            ubuf_ref[s, cc, HIST_ROWS:, :] = u[s * ts:(s + 1) * ts, cc * LANES:(cc + 1) * LANES]

    blk = CONV_ROW_STRIDE * SUBLANES
    for s in range(ns):
        for cc in range(n_slab):
            ch = slice(cc * LANES, (cc + 1) * LANES)
            bias = jnp.broadcast_to(cb_ref[:, ch], (SUBLANES, LANES))
            for rb in range(ts // blk):
                r0 = rb * blk
                accs = [bias] * CONV_ROW_STRIDE
                for j in range(CONV_WIDTH):
                    w = cw_ref[j:j + 1, ch]
                    for m in range(CONV_ROW_STRIDE):
                        tap = ubuf_ref[s, cc, pl.ds(r0 + m + j + HIST_PAD, SUBLANES,
                                                    stride=CONV_ROW_STRIDE), :]
                        accs[m] = accs[m] + w * tap
                for m in range(CONV_ROW_STRIDE):
                    c_ref[cc, pl.ds(s * ts + r0 + m, SUBLANES, stride=CONV_ROW_STRIDE), :] = accs[m]

    for s in range(ns):
        for cc in range(n_slab):
            tail = ubuf_ref[s, cc, ts:ts + HIST_ROWS, :]
            ubuf_ref[s, cc, 0:HIST_ROWS, :] = tail
            nbuf_ref[s, :, cc * LANES:(cc + 1) * LANES] = tail

    cs = [c_ref[cc] for cc in range(n_slab)]
    mu = jnp.sum(sum(cs[1:], cs[0]), axis=-1, keepdims=True) * (1.0 / c_conv)
    xcs = [c - mu for c in cs]
    sq = xcs[0] * xcs[0]
    for xc in xcs[1:]:
        sq = sq + xc * xc
    inv = lax.rsqrt(jnp.sum(sq, axis=-1, keepdims=True) * (1.0 / c_conv) + EPS)
    for cc in range(n_slab):
        ch = slice(cc * LANES, (cc + 1) * LANES)
        ln = xcs[cc] * inv * lnw_ref[:, ch] + lnb_ref[:, ch]
        act_ref[:, ch] = _silu(ln).astype(BF16)
    cp = jnp.dot(act_ref[...], wpw_ref[...], preferred_element_type=F32)
    g = jnp.dot(h, win_ref[:, 2 * c_conv:3 * c_conv], preferred_element_type=F32)
    y_ref[...] = (_silu(g) * cp).astype(BF16).reshape(ns, ts, c_conv)


def _conv_call(x, npre, win_c, cw, cb, lnw, lnb, wpw, buf0p, *, ns, ts):
    bsz, t_len, d_model = x.shape
    c_conv = wpw.shape[0]
    grid = (bsz // ns, t_len // ts)
    kern = functools.partial(_conv_kernel, ns=ns, ts=ts)
    return pl.pallas_call(
        kern,
        grid=grid,
        in_specs=[
            pl.BlockSpec((ns, ts, d_model), lambda b, t: (b, t, 0)),
            _resident(npre.shape),
            _resident(win_c.shape),
            _resident(cw.shape),
            _resident(cb.shape),
            _resident(lnw.shape),
            _resident(lnb.shape),
            _resident(wpw.shape),
            pl.BlockSpec((ns, HIST_ROWS, c_conv), lambda b, t: (b, 0, 0)),
        ],
        out_specs=[
            pl.BlockSpec((ns, ts, c_conv), lambda b, t: (b, t, 0)),
            pl.BlockSpec((ns, HIST_ROWS, c_conv), lambda b, t: (b, 0, 0)),
        ],
        out_shape=[
            jax.ShapeDtypeStruct((bsz, t_len, c_conv), BF16),
            jax.ShapeDtypeStruct((bsz, HIST_ROWS, c_conv), F32),
        ],
        scratch_shapes=[
            pltpu.VMEM((ns, c_conv // LANES, HIST_ROWS + ts, LANES), F32),
            pltpu.VMEM((c_conv // LANES, ns * ts, LANES), F32),
            pltpu.VMEM((ns * ts, c_conv), BF16),
        ],
        compiler_params=pltpu.CompilerParams(
            dimension_semantics=("arbitrary", "arbitrary"),
            vmem_limit_bytes=VMEM_LIMIT_BYTES),
        name="conv_branch",
    )(x, npre, win_c, cw, cb, lnw, lnb, wpw, buf0p)


def _ret_kernel(x_ref, npre_ref, win_ref, gnw_ref, cos_ref, sin_ref, dmat_ref, qdec_ref, kdec_ref,
                rdec_ref, r0_ref, y_ref, rout_ref, *, ns, ts):
    d_model = x_ref.shape[-1]
    c_ret = gnw_ref.shape[-1]
    hd = c_ret // N_RET_HEADS
    rows = ns * ts
    scale = hd ** -0.5

    @pl.when(pl.program_id(1) == 0)
    def _():
        rout_ref[...] = r0_ref[...]

    x = x_ref[...].reshape(rows, d_model)
    h = _rms_norm(x, npre_ref[...]).astype(BF16)
    zq = jnp.dot(h, win_ref[:, 0:c_ret], preferred_element_type=F32)
    zk = jnp.dot(h, win_ref[:, c_ret:2 * c_ret], preferred_element_type=F32)
    zv = jnp.dot(h, win_ref[:, 2 * c_ret:3 * c_ret], preferred_element_type=F32)
    zg = jnp.dot(h, win_ref[:, 3 * c_ret:4 * c_ret], preferred_element_type=F32)
    cos = cos_ref[0]
    sin = sin_ref[0]

    for hh in range(N_RET_HEADS):
        sl = slice(hh * hd, (hh + 1) * hd)
        q = zq[:, sl]
        k = zk[:, sl]
        q = q * cos + pltpu.roll(q, hd // 2, 1) * sin
        k = (k * cos + pltpu.roll(k, hd // 2, 1) * sin) * scale
        qb = q.astype(BF16)
        kb = k.astype(BF16)
        vb = zv[:, sl].astype(BF16)
        sc = lax.dot_general(qb, kb, (((1,), (1,)), ((), ())), preferred_element_type=F32)
        sc = (sc * dmat_ref[hh]).astype(BF16)
        o = jnp.dot(sc, vb, preferred_element_type=F32)
        qd = (q * qdec_ref[hh]).astype(BF16)
        kd = (k * kdec_ref[hh]).astype(BF16)
        inter = []
        for s in range(ns):
            rs = slice(s * ts, (s + 1) * ts)
            state = rout_ref[s, hh]
            inter.append(jnp.dot(qd[rs], state.astype(BF16), preferred_element_type=F32))
            kv = lax.dot_general(kd[rs], vb[rs], (((0,), (0,)), ((), ())), preferred_element_type=F32)
            rout_ref[s, hh] = state * rdec_ref[hh:hh + 1, :] + kv
        o = o + (inter[0] if ns == 1 else jnp.concatenate(inter, axis=0))
        mu = jnp.mean(o, axis=-1, keepdims=True)
        oc = o - mu
        on = oc * lax.rsqrt(jnp.mean(oc * oc, axis=-1, keepdims=True) + EPS)
        on = on * gnw_ref[:, sl]
        y_ref[:, :, sl] = (_silu(zg[:, sl]) * on).astype(BF16).reshape(ns, ts, hd)


def _ret_call(x, npre, win_r, gnw, cos_t, sin_t, dmat, qdec, kdec, rdec, r0, *, ns, ts):
    bsz, t_len, d_model = x.shape
    c_ret = gnw.shape[-1]
    hd = c_ret // N_RET_HEADS
    rows = ns * ts
    grid = (bsz // ns, t_len // ts)
    kern = functools.partial(_ret_kernel, ns=ns, ts=ts)
    return pl.pallas_call(
        kern,
        grid=grid,
        in_specs=[
            pl.BlockSpec((ns, ts, d_model), lambda b, t: (b, t, 0)),
            _resident(npre.shape),
            _resident(win_r.shape),
            _resident(gnw.shape),
            pl.BlockSpec((1, rows, hd), lambda b, t: (t, 0, 0)),
            pl.BlockSpec((1, rows, hd), lambda b, t: (t, 0, 0)),
            _resident(dmat.shape),
            _resident(qdec.shape),
            _resident(kdec.shape),
            _resident(rdec.shape),
            pl.BlockSpec((ns, N_RET_HEADS, hd, hd), lambda b, t: (b, 0, 0, 0)),
        ],
        out_specs=[
            pl.BlockSpec((ns, ts, c_ret), lambda b, t: (b, t, 0)),
            pl.BlockSpec((ns, N_RET_HEADS, hd, hd), lambda b, t: (b, 0, 0, 0)),
        ],
        out_shape=[
            jax.ShapeDtypeStruct((bsz, t_len, c_ret), BF16),
            jax.ShapeDtypeStruct((bsz, N_RET_HEADS, hd, hd), F32),
        ],
        compiler_params=pltpu.CompilerParams(
            dimension_semantics=("arbitrary", "arbitrary"),
            vmem_limit_bytes=VMEM_LIMIT_BYTES),
        name="retention_branch",
    )(x, npre, win_r, gnw, cos_t, sin_t, dmat, qdec, kdec, rdec, r0)


def _out_kernel(yc_ref, yr_ref, wout_ref, x_ref, npost_ref, o_ref):
    c_conv = yc_ref.shape[-1]
    y = jnp.dot(yc_ref[...], wout_ref[0:c_conv, :], preferred_element_type=F32)
    y = y + jnp.dot(yr_ref[...], wout_ref[c_conv:, :], preferred_element_type=F32)
    o_ref[...] = x_ref[...] + _rms_norm(y, npost_ref[...])


def _out_call(yc, yr, wout, x, npost, *, rows):
    n_tok, d_model = x.shape
    return pl.pallas_call(
        _out_kernel,
        grid=(n_tok // rows,),
        in_specs=[
            pl.BlockSpec((rows, yc.shape[-1]), lambda i: (i, 0)),
            pl.BlockSpec((rows, yr.shape[-1]), lambda i: (i, 0)),
            _resident(wout.shape),
            pl.BlockSpec((rows, d_model), lambda i: (i, 0)),
            _resident(npost.shape),
        ],
        out_specs=pl.BlockSpec((rows, d_model), lambda i: (i, 0)),
        out_shape=jax.ShapeDtypeStruct((n_tok, d_model), F32),
        compiler_params=pltpu.CompilerParams(
            dimension_semantics=("arbitrary",),
            vmem_limit_bytes=VMEM_LIMIT_BYTES),
        name="out_proj",
    )(yc, yr, wout, x, npost)


def _retention_tables(pos0, t_len, ns, ts, hd):
    n_t = t_len // ts
    half = hd // 2
    inv_freq = ROPE_BASE ** (-jnp.arange(half, dtype=F32) / half)
    pos = pos0 + jnp.arange(t_len)
    ang = pos.astype(F32)[:, None] * inv_freq[None, :]
    cos = jnp.cos(ang)
    sin = jnp.sin(ang)
    cos2 = jnp.concatenate([cos, cos], axis=-1).reshape(n_t, 1, ts, hd)
    sin2 = jnp.concatenate([-sin, sin], axis=-1).reshape(n_t, 1, ts, hd)
    cos_t = jnp.broadcast_to(cos2, (n_t, ns, ts, hd)).reshape(n_t, ns * ts, hd)
    sin_t = jnp.broadcast_to(sin2, (n_t, ns, ts, hd)).reshape(n_t, ns * ts, hd)

    log_g = jnp.log1p(-jnp.exp2(-5.0 - jnp.arange(N_RET_HEADS, dtype=F32)))
    i = jnp.arange(ts, dtype=F32)
    lg = log_g[:, None]
    qdec = jnp.exp((i + 1.0)[None, :] * lg)
    kdec = jnp.exp((ts - 1.0 - i)[None, :] * lg)
    rdec = jnp.exp(ts * log_g)

    def rows_table(v):
        v = jnp.broadcast_to(v[:, None, :, None], (N_RET_HEADS, ns, ts, hd))
        return v.reshape(N_RET_HEADS, ns * ts, hd)

    dist = jnp.abs(i[:, None] - i[None, :])
    dm = jnp.exp(dist[None] * log_g[:, None, None])
    chunk_id = jnp.arange(ts) // CHUNK
    causal = (chunk_id[None, :] <= chunk_id[:, None]).astype(F32)
    dm = dm * causal[None]
    seq_eye = jnp.eye(ns, dtype=F32)
    dmat = (seq_eye[None, :, None, :, None] * dm[:, None, :, None, :]).reshape(
        N_RET_HEADS, ns * ts, ns * ts)
    rdec_t = jnp.broadcast_to(rdec[:, None], (N_RET_HEADS, hd))
    return cos_t, sin_t, dmat, rows_table(qdec), rows_table(kdec), rdec_t


def _trunk(x, pos0, conv_bufs, r0s, params, *, ns, ts, out_rows):
    (norm_pre, w_in_c, w_in_r, conv_w, conv_b, conv_ln_w, conv_ln_b, w_pw, ret_gn_w, w_out,
     norm_post) = params
    depth = norm_pre.shape[0]
    bsz, t_len, d_model = x.shape
    c_ret = ret_gn_w.shape[-1]
    hd = c_ret // N_RET_HEADS
    tables = _retention_tables(pos0, t_len, ns, ts, hd)
    bufs_p = jnp.pad(conv_bufs, ((0, 0), (0, 0), (HIST_PAD, 0), (0, 0)))
    bufs, states = [], []
    for l in range(depth):
        npre = norm_pre[l][None, :]
        y_conv, nbuf = _conv_call(x, npre, w_in_c[l], conv_w[l], conv_b[l][None, :],
                                  conv_ln_w[l][None, :], conv_ln_b[l][None, :], w_pw[l], bufs_p[l],
                                  ns=ns, ts=ts)
        y_ret, r_new = _ret_call(x, npre, w_in_r[l], ret_gn_w[l][None, :], *tables, r0s[l],
                                 ns=ns, ts=ts)
        x = _out_call(y_conv.reshape(bsz * t_len, -1), y_ret.reshape(bsz * t_len, -1), w_out[l],
                      x.reshape(bsz * t_len, d_model), norm_post[l][None, :],
                      rows=out_rows).reshape(bsz, t_len, d_model)
        bufs.append(nbuf[:, HIST_PAD:, :])
        states.append(r_new)
    return x, jnp.stack(bufs), jnp.stack(states)


def kernel(x_prompt, x_sample, cache_conv, state_ret, norm_pre, w_in, conv_w, conv_b, conv_ln_w,
           conv_ln_b, w_pw, ret_gn_w, w_out, norm_post):
    depth = norm_pre.shape[0]
    bsz, t_len, _ = x_prompt.shape
    c_conv = w_pw.shape[-1]
    c_ret = ret_gn_w.shape[-1]
    hd = c_ret // N_RET_HEADS
    assert t_len >= CONV_STATE and x_sample.shape[1] >= CONV_STATE

    w_in_b = w_in.astype(BF16)
    params = (norm_pre, w_in_b[:, :, :3 * c_conv], w_in_b[:, :, 3 * c_conv:], conv_w, conv_b,
              conv_ln_w, conv_ln_b, w_pw.astype(BF16), ret_gn_w, w_out.astype(BF16), norm_post)

    zero_bufs = jnp.zeros((depth, bsz, CONV_STATE, c_conv), F32)
    zero_r = jnp.zeros((depth, bsz, N_RET_HEADS, hd, hd), F32)
    ts_p = min(256, t_len)
    y_p, conv_p, ret_p = _trunk(x_prompt, 0, zero_bufs, zero_r, params, ns=1, ts=ts_p,
                                out_rows=min(512, bsz * t_len))
    sb, st, _ = x_sample.shape
    ns_s = min(4, sb)
    y_s, conv_s, ret_s = _trunk(x_sample, PAST_LEN, cache_conv, state_ret.astype(F32), params,
                                ns=ns_s, ts=st, out_rows=min(512, sb * st))
    return (y_p, y_s, conv_p, ret_p, conv_s, ret_s)
```

```python
import functools

import jax
import jax.numpy as jnp
from jax import lax
from jax.experimental import pallas as pl
from jax.experimental.pallas import tpu as pltpu

CHUNK = 64
CONV_WIDTH = 31
CONV_STATE = CONV_WIDTH - 1
N_RET_HEADS = 8
ROPE_BASE = 10000.0
EPS = 1e-6

LANES = 128
SUBLANES = 8
HIST_ROWS = 32
HIST_PAD = HIST_ROWS - CONV_STATE
CONV_ROW_STRIDE = 4
PAST_LEN = 2048
GLU_CHUNK = 256
VMEM_LIMIT_BYTES = 56 * 1024 * 1024

F32 = jnp.float32
BF16 = jnp.bfloat16


def _rms_norm(xf, w):
    y = xf * lax.rsqrt(jnp.mean(xf * xf, axis=-1, keepdims=True) + EPS)
    return y * w


def _silu(x):
    return x * jax.nn.sigmoid(x)


def _resident(shape):
    nd = len(shape)
    return pl.BlockSpec(shape, lambda *_: (0,) * nd, pipeline_mode=pl.Buffered(1))


def _mixer_kernel(h_ref, win_ref, cw_ref, cb_ref, lnw_ref, lnb_ref, wpw_ref, gnw_ref, cos_ref, sin_ref,
                  dmat_ref, qdec_ref, kdec_ref, rdec_ref, buf0_ref, r0_ref,
                  y_ref, nbuf_ref, rout_ref, ubuf_ref, c_ref, act_ref, *, ns, ts):
    d_model = h_ref.shape[-1]
    c_conv = wpw_ref.shape[0]
    c_ret = gnw_ref.shape[-1]
    n_slab = c_conv // LANES
    hd = c_ret // N_RET_HEADS
    rows = ns * ts
    scale = hd ** -0.5
    ret0 = 3 * c_conv

    @pl.when(pl.program_id(1) == 0)
    def _():
        rout_ref[...] = r0_ref[...]
        for s in range(ns):
            for cc in range(n_slab):
                ubuf_ref[s, cc, 0:HIST_ROWS, :] = buf0_ref[s, :, cc * LANES:(cc + 1) * LANES]

    h = h_ref[...].reshape(rows, d_model)

    def proj(col0, width):
        return jnp.dot(h, win_ref[:, col0:col0 + width], preferred_element_type=F32)

    for c2 in range(c_conv // GLU_CHUNK):
        a = proj(c2 * GLU_CHUNK, GLU_CHUNK)
        b = proj(c_conv + c2 * GLU_CHUNK, GLU_CHUNK)
        u = a * jax.nn.sigmoid(b)
        for s in range(ns):
            for k in range(GLU_CHUNK // LANES):
                ubuf_ref[s, c2 * (GLU_CHUNK // LANES) + k, HIST_ROWS:, :] = (
                    u[s * ts:(s + 1) * ts, k * LANES:(k + 1) * LANES])
    g_conv = proj(2 * c_conv, c_conv)
    zq = proj(ret0, c_ret)
    zk = proj(ret0 + c_ret, c_ret)
    zv = proj(ret0 + 2 * c_ret, c_ret)
    zg = proj(ret0 + 3 * c_ret, c_ret)

    blk = CONV_ROW_STRIDE * SUBLANES
    for s in range(ns):
        for cc in range(n_slab):
            ch = slice(cc * LANES, (cc + 1) * LANES)
            bias = jnp.broadcast_to(cb_ref[:, ch], (SUBLANES, LANES))
            for rb in range(ts // blk):
                r0 = rb * blk
                accs = [bias] * CONV_ROW_STRIDE
                for j in range(CONV_WIDTH):
                    w = cw_ref[j:j + 1, ch]
                    for m in range(CONV_ROW_STRIDE):
                        tap = ubuf_ref[s, cc, pl.ds(r0 + m + j + HIST_PAD, SUBLANES,
                                                    stride=CONV_ROW_STRIDE), :]
                        accs[m] = accs[m] + w * tap
                for m in range(CONV_ROW_STRIDE):
                    c_ref[cc, pl.ds(s * ts + r0 + m, SUBLANES, stride=CONV_ROW_STRIDE), :] = accs[m]

    for s in range(ns):
        for cc in range(n_slab):
            tail = ubuf_ref[s, cc, ts:ts + HIST_ROWS, :]
            ubuf_ref[s, cc, 0:HIST_ROWS, :] = tail
            nbuf_ref[s, :, cc * LANES:(cc + 1) * LANES] = tail

    cs = [c_ref[cc] for cc in range(n_slab)]
    mu = jnp.sum(sum(cs[1:], cs[0]), axis=-1, keepdims=True) * (1.0 / c_conv)
    xcs = [c - mu for c in cs]
    sq = xcs[0] * xcs[0]
    for xc in xcs[1:]:
        sq = sq + xc * xc
    inv = lax.rsqrt(jnp.sum(sq, axis=-1, keepdims=True) * (1.0 / c_conv) + EPS)
    for cc in range(n_slab):
        ch = slice(cc * LANES, (cc + 1) * LANES)
        ln = xcs[cc] * inv * lnw_ref[:, ch] + lnb_ref[:, ch]
        act_ref[:, ch] = _silu(ln).astype(BF16)

    cos = cos_ref[0]
    sin = sin_ref[0]
    for hh in range(N_RET_HEADS):
        sl = slice(hh * hd, (hh + 1) * hd)
        q = zq[:, sl]
        k = zk[:, sl]
        q = q * cos + pltpu.roll(q, hd // 2, 1) * sin
        k = (k * cos + pltpu.roll(k, hd // 2, 1) * sin) * scale
        qb = q.astype(BF16)
        kb = k.astype(BF16)
        vb = zv[:, sl].astype(BF16)
        sc = lax.dot_general(qb, kb, (((1,), (1,)), ((), ())), preferred_element_type=F32)
        sc = (sc * dmat_ref[hh]).astype(BF16)
        o = jnp.dot(sc, vb, preferred_element_type=F32)
        qd = (q * qdec_ref[hh]).astype(BF16)
        kd = (k * kdec_ref[hh]).astype(BF16)
        inter = []
        for s in range(ns):
            rs = slice(s * ts, (s + 1) * ts)
            state = rout_ref[s, hh]
            inter.append(jnp.dot(qd[rs], state.astype(BF16), preferred_element_type=F32))
            kv = lax.dot_general(kd[rs], vb[rs], (((0,), (0,)), ((), ())), preferred_element_type=F32)
            rout_ref[s, hh] = state * rdec_ref[hh:hh + 1, :] + kv
        o = o + (inter[0] if ns == 1 else jnp.concatenate(inter, axis=0))
        mu = jnp.mean(o, axis=-1, keepdims=True)
        oc = o - mu
        on = oc * lax.rsqrt(jnp.mean(oc * oc, axis=-1, keepdims=True) + EPS)
        on = on * gnw_ref[:, sl]
        y_ref[:, :, c_conv + hh * hd:c_conv + (hh + 1) * hd] = (
            (_silu(zg[:, sl]) * on).astype(BF16).reshape(ns, ts, hd))

    cp = jnp.dot(act_ref[...], wpw_ref[...], preferred_element_type=F32)
    y_ref[:, :, 0:c_conv] = (_silu(g_conv) * cp).astype(BF16).reshape(ns, ts, c_conv)


def _mixer_call(h, win, cw, cb, lnw, lnb, wpw, gnw, cos_t, sin_t, dmat, qdec, kdec, rdec, buf0p, r0,
                *, ns, ts):
    bsz, t_len, d_model = h.shape
    c_conv = wpw.shape[0]
    c_ret = gnw.shape[-1]
    hd = c_ret // N_RET_HEADS
    rows = ns * ts
    assert ts % CHUNK == 0 and ts % (CONV_ROW_STRIDE * SUBLANES) == 0 and ts >= HIST_ROWS
    assert bsz % ns == 0 and t_len % ts == 0 and c_conv % GLU_CHUNK == 0
    kern = functools.partial(_mixer_kernel, ns=ns, ts=ts)
    state_spec = pl.BlockSpec((ns, N_RET_HEADS, hd, hd), lambda b, t: (b, 0, 0, 0))
    hist_spec = pl.BlockSpec((ns, HIST_ROWS, c_conv), lambda b, t: (b, 0, 0))
    return pl.pallas_call(
        kern,
        grid=(bsz // ns, t_len // ts),
        in_specs=[
            pl.BlockSpec((ns, ts, d_model), lambda b, t: (b, t, 0)),
            _resident(win.shape),
            _resident(cw.shape),
            _resident(cb.shape),
            _resident(lnw.shape),
            _resident(lnb.shape),
            _resident(wpw.shape),
            _resident(gnw.shape),
            pl.BlockSpec((1, rows, hd), lambda b, t: (t, 0, 0)),
            pl.BlockSpec((1, rows, hd), lambda b, t: (t, 0, 0)),
            _resident(dmat.shape),
            _resident(qdec.shape),
            _resident(kdec.shape),
            _resident(rdec.shape),
            hist_spec,
            state_spec,
        ],
        out_specs=[
            pl.BlockSpec((ns, ts, c_conv + c_ret), lambda b, t: (b, t, 0)),
            hist_spec,
            state_spec,
        ],
        out_shape=[
            jax.ShapeDtypeStruct((bsz, t_len, c_conv + c_ret), BF16),
            jax.ShapeDtypeStruct((bsz, HIST_ROWS, c_conv), F32),
            jax.ShapeDtypeStruct((bsz, N_RET_HEADS, hd, hd), F32),
        ],
        scratch_shapes=[
            pltpu.VMEM((ns, c_conv // LANES, HIST_ROWS + ts, LANES), F32),
            pltpu.VMEM((c_conv // LANES, rows, LANES), F32),
            pltpu.VMEM((rows, c_conv), BF16),
        ],
        compiler_params=pltpu.CompilerParams(
            dimension_semantics=("arbitrary", "arbitrary"),
            vmem_limit_bytes=VMEM_LIMIT_BYTES),
        name="mixer",
    )(h, win, cw, cb, lnw, lnb, wpw, gnw, cos_t, sin_t, dmat, qdec, kdec, rdec, buf0p, r0)


def _norm_kernel(x_ref, w_ref, h_ref):
    h_ref[...] = _rms_norm(x_ref[...], w_ref[...]).astype(BF16)


def _norm_call(x, w, *, rows):
    n_tok, d_model = x.shape
    return pl.pallas_call(
        _norm_kernel,
        grid=(n_tok // rows,),
        in_specs=[pl.BlockSpec((rows, d_model), lambda i: (i, 0)), _resident(w.shape)],
        out_specs=pl.BlockSpec((rows, d_model), lambda i: (i, 0)),
        out_shape=jax.ShapeDtypeStruct((n_tok, d_model), BF16),
        compiler_params=pltpu.CompilerParams(
            dimension_semantics=("arbitrary",),
            vmem_limit_bytes=VMEM_LIMIT_BYTES),
        name="pre_norm",
    )(x, w)


def _out_kernel(y_ref, wout_ref, x_ref, npost_ref, *refs, emit_next):
    y = jnp.dot(y_ref[...], wout_ref[...], preferred_element_type=F32)
    x_new = x_ref[...] + _rms_norm(y, npost_ref[...])
    if emit_next:
        nnext_ref, o_ref, h_ref = refs
        h_ref[...] = _rms_norm(x_new, nnext_ref[...]).astype(BF16)
    else:
        (o_ref,) = refs
    o_ref[...] = x_new


def _out_call(y, wout, x, npost, nnext, *, rows):
    n_tok, d_model = x.shape
    emit_next = nnext is not None
    row_spec = pl.BlockSpec((rows, d_model), lambda i: (i, 0))
    in_specs = [
        pl.BlockSpec((rows, y.shape[-1]), lambda i: (i, 0)),
        _resident(wout.shape),
        row_spec,
        _resident(npost.shape),
    ]
    args = [y, wout, x, npost]
    out_specs = [row_spec]
    out_shape = [jax.ShapeDtypeStruct((n_tok, d_model), F32)]
    if emit_next:
        in_specs.append(_resident(nnext.shape))
        args.append(nnext)
        out_specs.append(row_spec)
        out_shape.append(jax.ShapeDtypeStruct((n_tok, d_model), BF16))
    return pl.pallas_call(
        functools.partial(_out_kernel, emit_next=emit_next),
        grid=(n_tok // rows,),
        in_specs=in_specs,
        out_specs=out_specs,
        out_shape=out_shape,
        compiler_params=pltpu.CompilerParams(
            dimension_semantics=("arbitrary",),
            vmem_limit_bytes=VMEM_LIMIT_BYTES),
        name="out_proj",
    )(*args)


def _retention_tables(pos0, t_len, ns, ts, hd):
    n_t = t_len // ts
    half = hd // 2
    inv_freq = ROPE_BASE ** (-jnp.arange(half, dtype=F32) / half)
    pos = pos0 + jnp.arange(t_len)
    ang = pos.astype(F32)[:, None] * inv_freq[None, :]
    cos = jnp.cos(ang)
    sin = jnp.sin(ang)
    cos2 = jnp.concatenate([cos, cos], axis=-1).reshape(n_t, 1, ts, hd)
    sin2 = jnp.concatenate([-sin, sin], axis=-1).reshape(n_t, 1, ts, hd)
    cos_t = jnp.broadcast_to(cos2, (n_t, ns, ts, hd)).reshape(n_t, ns * ts, hd)
    sin_t = jnp.broadcast_to(sin2, (n_t, ns, ts, hd)).reshape(n_t, ns * ts, hd)

    log_g = jnp.log1p(-jnp.exp2(-5.0 - jnp.arange(N_RET_HEADS, dtype=F32)))
    i = jnp.arange(ts, dtype=F32)
    lg = log_g[:, None]
    qdec = jnp.exp((i + 1.0)[None, :] * lg)
    kdec = jnp.exp((ts - 1.0 - i)[None, :] * lg)
    rdec = jnp.exp(ts * log_g)

    def rows_table(v):
        v = jnp.broadcast_to(v[:, None, :, None], (N_RET_HEADS, ns, ts, hd))
        return v.reshape(N_RET_HEADS, ns * ts, hd)

    dist = jnp.abs(i[:, None] - i[None, :])
    dm = jnp.exp(dist[None] * log_g[:, None, None])
    chunk_id = jnp.arange(ts) // CHUNK
    causal = (chunk_id[None, :] <= chunk_id[:, None]).astype(F32)
    dm = dm * causal[None]
    seq_eye = jnp.eye(ns, dtype=F32)
    dmat = (seq_eye[None, :, None, :, None] * dm[:, None, :, None, :]).reshape(
        N_RET_HEADS, ns * ts, ns * ts)
    rdec_t = jnp.broadcast_to(rdec[:, None], (N_RET_HEADS, hd))
    return cos_t, sin_t, dmat, rows_table(qdec), rows_table(kdec), rdec_t


def _trunk(x, pos0, conv_bufs, r0s, params, *, ns, ts, out_rows):
    (norm_pre, w_in, conv_w, conv_b, conv_ln_w, conv_ln_b, w_pw, ret_gn_w, w_out, norm_post) = params
    depth = norm_pre.shape[0]
    bsz, t_len, d_model = x.shape
    n_tok = bsz * t_len
    hd = ret_gn_w.shape[-1] // N_RET_HEADS
    tables = _retention_tables(pos0, t_len, ns, ts, hd)
    bufs_p = jnp.pad(conv_bufs, ((0, 0), (0, 0), (HIST_PAD, 0), (0, 0)))
    x = x.reshape(n_tok, d_model)
    h = _norm_call(x, norm_pre[0][None, :], rows=out_rows)
    bufs, states = [], []
    for l in range(depth):
        y, nbuf, r_new = _mixer_call(
            h.reshape(bsz, t_len, d_model), w_in[l], conv_w[l], conv_b[l][None, :],
            conv_ln_w[l][None, :], conv_ln_b[l][None, :], w_pw[l], ret_gn_w[l][None, :], *tables,
            bufs_p[l], r0s[l], ns=ns, ts=ts)
        nnext = norm_pre[l + 1][None, :] if l + 1 < depth else None
        outs = _out_call(y.reshape(n_tok, -1), w_out[l], x, norm_post[l][None, :], nnext, rows=out_rows)
        x = outs[0]
        h = outs[1] if nnext is not None else None
        bufs.append(nbuf[:, HIST_PAD:, :])
        states.append(r_new)
    return x.reshape(bsz, t_len, d_model), jnp.stack(bufs), jnp.stack(states)


def kernel(x_prompt, x_sample, cache_conv, state_ret, norm_pre, w_in, conv_w, conv_b, conv_ln_w,
           conv_ln_b, w_pw, ret_gn_w, w_out, norm_post):
    depth = norm_pre.shape[0]
    bsz, t_len, _ = x_prompt.shape
    c_conv = w_pw.shape[-1]
    hd = ret_gn_w.shape[-1] // N_RET_HEADS
    assert t_len >= CONV_STATE and x_sample.shape[1] >= CONV_STATE

    params = (norm_pre, w_in.astype(BF16), conv_w, conv_b, conv_ln_w, conv_ln_b, w_pw.astype(BF16),
              ret_gn_w, w_out.astype(BF16), norm_post)

    zero_bufs = jnp.zeros((depth, bsz, CONV_STATE, c_conv), F32)
    zero_r = jnp.zeros((depth, bsz, N_RET_HEADS, hd, hd), F32)
    ts_p = min(256, t_len)
    y_p, conv_p, ret_p = _trunk(x_prompt, 0, zero_bufs, zero_r, params, ns=1, ts=ts_p,
                                out_rows=min(512, bsz * t_len))
    sb, st, _ = x_sample.shape
    ns_s = min(2, sb)
    y_s, conv_s, ret_s = _trunk(x_sample, PAST_LEN, cache_conv, state_ret.astype(F32), params,
                                ns=ns_s, ts=st, out_rows=min(512, sb * st))
    return (y_p, y_s, conv_p, ret_p, conv_s, ret_s)
```

```python
import functools

import jax
import jax.numpy as jnp
from jax import lax
from jax.experimental import pallas as pl
from jax.experimental.pallas import tpu as pltpu

CHUNK = 64
CONV_WIDTH = 31
CONV_STATE = CONV_WIDTH - 1
N_RET_HEADS = 8
ROPE_BASE = 10000.0
EPS = 1e-6

LANES = 128
SUBLANES = 8
HIST_ROWS = 32
HIST_PAD = HIST_ROWS - CONV_STATE
CONV_ROW_STRIDE = 4
PAST_LEN = 2048
GLU_CHUNK = 256
OUT_PART_ROWS = 128
VMEM_LIMIT_BYTES = 56 * 1024 * 1024
SAMPLE_VMEM_LIMIT_BYTES = 60 * 1024 * 1024

F32 = jnp.float32
BF16 = jnp.bfloat16


def _rms_norm(xf, w):
    y = xf * lax.rsqrt(jnp.mean(xf * xf, axis=-1, keepdims=True) + EPS)
    return y * w


def _silu(x):
    return x * jax.nn.sigmoid(x)


def _resident(shape):
    nd = len(shape)
    return pl.BlockSpec(shape, lambda *_: (0,) * nd, pipeline_mode=pl.Buffered(1))


def _layer_resident(stacked_shape, layer):
    nd = len(stacked_shape) - 1
    return pl.BlockSpec((None,) + tuple(stacked_shape[1:]), lambda *_: (layer,) + (0,) * nd,
                        pipeline_mode=pl.Buffered(1))


def _mixer_kernel(in_ref, *refs, ns, ts, normalize):
    if normalize:
        npre_ref, *refs = refs
    (win_ref, cw_ref, cb_ref, lnw_ref, lnb_ref, wpw_ref, gnw_ref, cos_ref, sin_ref, dmat_ref, qdec_ref,
     kdec_ref, rdec_ref, buf0_ref, r0_ref, y_ref, nbuf_ref, rout_ref, ubuf_ref, c_ref, act_ref) = refs
    d_model = in_ref.shape[-1]
    c_conv = wpw_ref.shape[0]
    c_ret = gnw_ref.shape[-1]
    n_slab = c_conv // LANES
    hd = c_ret // N_RET_HEADS
    rows = ns * ts
    scale = hd ** -0.5
    ret0 = 3 * c_conv

    @pl.when(pl.program_id(1) == 0)
    def _():
        rout_ref[...] = r0_ref[...]
        for s in range(ns):
            for cc in range(n_slab):
                ubuf_ref[s, cc, 0:HIST_ROWS, :] = buf0_ref[s, :, cc * LANES:(cc + 1) * LANES]

    h = in_ref[...].reshape(rows, d_model)
    if normalize:
        h = _rms_norm(h, npre_ref[...]).astype(BF16)

    def proj(col0, width):
        return jnp.dot(h, win_ref[:, col0:col0 + width], preferred_element_type=F32)

    for c2 in range(c_conv // GLU_CHUNK):
        a = proj(c2 * GLU_CHUNK, GLU_CHUNK)
        b = proj(c_conv + c2 * GLU_CHUNK, GLU_CHUNK)
        u = a * jax.nn.sigmoid(b)
        for s in range(ns):
            for k in range(GLU_CHUNK // LANES):
                ubuf_ref[s, c2 * (GLU_CHUNK // LANES) + k, HIST_ROWS:, :] = (
                    u[s * ts:(s + 1) * ts, k * LANES:(k + 1) * LANES])
    g_conv = proj(2 * c_conv, c_conv)
    zq = proj(ret0, c_ret)
    zk = proj(ret0 + c_ret, c_ret)
    zv = proj(ret0 + 2 * c_ret, c_ret)
    zg = proj(ret0 + 3 * c_ret, c_ret)

    blk = CONV_ROW_STRIDE * SUBLANES
    for s in range(ns):
        for cc in range(n_slab):
            ch = slice(cc * LANES, (cc + 1) * LANES)
            bias = jnp.broadcast_to(cb_ref[:, ch], (SUBLANES, LANES))
            for rb in range(ts // blk):
                r0 = rb * blk
                accs = [bias] * CONV_ROW_STRIDE
                for j in range(CONV_WIDTH):
                    w = cw_ref[j:j + 1, ch]
                    for m in range(CONV_ROW_STRIDE):
                        tap = ubuf_ref[s, cc, pl.ds(r0 + m + j + HIST_PAD, SUBLANES,
                                                    stride=CONV_ROW_STRIDE), :]
                        accs[m] = accs[m] + w * tap
                for m in range(CONV_ROW_STRIDE):
                    c_ref[cc, pl.ds(s * ts + r0 + m, SUBLANES, stride=CONV_ROW_STRIDE), :] = accs[m]

    for s in range(ns):
        for cc in range(n_slab):
            tail = ubuf_ref[s, cc, ts:ts + HIST_ROWS, :]
            ubuf_ref[s, cc, 0:HIST_ROWS, :] = tail
            nbuf_ref[s, :, cc * LANES:(cc + 1) * LANES] = tail

    cs = [c_ref[cc] for cc in range(n_slab)]
    mu = jnp.sum(sum(cs[1:], cs[0]), axis=-1, keepdims=True) * (1.0 / c_conv)
    xcs = [c - mu for c in cs]
    sq = xcs[0] * xcs[0]
    for xc in xcs[1:]:
        sq = sq + xc * xc
    inv = lax.rsqrt(jnp.sum(sq, axis=-1, keepdims=True) * (1.0 / c_conv) + EPS)
    for cc in range(n_slab):
        ch = slice(cc * LANES, (cc + 1) * LANES)
        ln = xcs[cc] * inv * lnw_ref[:, ch] + lnb_ref[:, ch]
        act_ref[:, ch] = _silu(ln).astype(BF16)

    cos = cos_ref[0]
    sin = sin_ref[0]
    for hh in range(N_RET_HEADS):
        sl = slice(hh * hd, (hh + 1) * hd)
        q = zq[:, sl]
        k = zk[:, sl]
        q = q * cos + pltpu.roll(q, hd // 2, 1) * sin
        k = (k * cos + pltpu.roll(k, hd // 2, 1) * sin) * scale
        qb = q.astype(BF16)
        kb = k.astype(BF16)
        vb = zv[:, sl].astype(BF16)
        sc = lax.dot_general(qb, kb, (((1,), (1,)), ((), ())), preferred_element_type=F32)
        sc = (sc * dmat_ref[hh]).astype(BF16)
        o = jnp.dot(sc, vb, preferred_element_type=F32)
        qd = (q * qdec_ref[hh]).astype(BF16)
        kd = (k * kdec_ref[hh]).astype(BF16)
        inter = []
        for s in range(ns):
            rs = slice(s * ts, (s + 1) * ts)
            state = rout_ref[s, hh]
            inter.append(jnp.dot(qd[rs], state.astype(BF16), preferred_element_type=F32))
            kv = lax.dot_general(kd[rs], vb[rs], (((0,), (0,)), ((), ())), preferred_element_type=F32)
            rout_ref[s, hh] = state * rdec_ref[hh:hh + 1, :] + kv
        o = o + (inter[0] if ns == 1 else jnp.concatenate(inter, axis=0))
        mu = jnp.mean(o, axis=-1, keepdims=True)
        oc = o - mu
        on = oc * lax.rsqrt(jnp.mean(oc * oc, axis=-1, keepdims=True) + EPS)
        on = on * gnw_ref[:, sl]
        y_ref[:, :, c_conv + hh * hd:c_conv + (hh + 1) * hd] = (
            (_silu(zg[:, sl]) * on).astype(BF16).reshape(ns, ts, hd))

    cp = jnp.dot(act_ref[...], wpw_ref[...], preferred_element_type=F32)
    y_ref[:, :, 0:c_conv] = (_silu(g_conv) * cp).astype(BF16).reshape(ns, ts, c_conv)


def _mixer_call(xin, npre, layer, weights, tables, bufs_p, r0s, *, ns, ts, vmem_limit):
    win, cw, cb, lnw, lnb, wpw, gnw = weights
    cos_t, sin_t, dmat, qdec, kdec, rdec = tables
    bsz, t_len, d_model = xin.shape
    c_conv = wpw.shape[-1]
    c_ret = gnw.shape[-1]
    hd = c_ret // N_RET_HEADS
    rows = ns * ts
    normalize = npre is not None
    assert ts % CHUNK == 0 and ts % (CONV_ROW_STRIDE * SUBLANES) == 0 and ts >= HIST_ROWS
    assert bsz % ns == 0 and t_len % ts == 0 and c_conv % GLU_CHUNK == 0
    kern = functools.partial(_mixer_kernel, ns=ns, ts=ts, normalize=normalize)
    state_in = pl.BlockSpec((None, ns, N_RET_HEADS, hd, hd), lambda b, t: (layer, b, 0, 0, 0),
                            pipeline_mode=pl.Buffered(1))
    hist_in = pl.BlockSpec((None, ns, HIST_ROWS, c_conv), lambda b, t: (layer, b, 0, 0),
                           pipeline_mode=pl.Buffered(1))
    in_specs = [pl.BlockSpec((ns, ts, d_model), lambda b, t: (b, t, 0))]
    args = [xin]
    if normalize:
        in_specs.append(_layer_resident(npre.shape, layer))
        args.append(npre)
    in_specs += [_layer_resident(w.shape, layer) for w in weights]
    in_specs += [
        pl.BlockSpec((1, rows, hd), lambda b, t: (t, 0, 0)),
        pl.BlockSpec((1, rows, hd), lambda b, t: (t, 0, 0)),
        _resident(dmat.shape),
        _resident(qdec.shape),
        _resident(kdec.shape),
        _resident(rdec.shape),
        hist_in,
        state_in,
    ]
    args += [*weights, cos_t, sin_t, dmat, qdec, kdec, rdec, bufs_p, r0s]
    return pl.pallas_call(
        kern,
        grid=(bsz // ns, t_len // ts),
        in_specs=in_specs,
        out_specs=[
            pl.BlockSpec((ns, ts, c_conv + c_ret), lambda b, t: (b, t, 0)),
            pl.BlockSpec((ns, HIST_ROWS, c_conv), lambda b, t: (b, 0, 0)),
            pl.BlockSpec((ns, N_RET_HEADS, hd, hd), lambda b, t: (b, 0, 0, 0)),
        ],
        out_shape=[
            jax.ShapeDtypeStruct((bsz, t_len, c_conv + c_ret), BF16),
            jax.ShapeDtypeStruct((bsz, HIST_ROWS, c_conv), F32),
            jax.ShapeDtypeStruct((bsz, N_RET_HEADS, hd, hd), F32),
        ],
        scratch_shapes=[
            pltpu.VMEM((ns, c_conv // LANES, HIST_ROWS + ts, LANES), F32),
            pltpu.VMEM((c_conv // LANES, rows, LANES), F32),
            pltpu.VMEM((rows, c_conv), BF16),
        ],
        compiler_params=pltpu.CompilerParams(
            dimension_semantics=("arbitrary", "arbitrary"),
            vmem_limit_bytes=vmem_limit),
        name="mixer",
    )(*args)


def _out_kernel(y_ref, wout_ref, x_ref, npost_ref, *refs, emit_next):
    if emit_next:
        nnext_ref, o_ref, h_ref = refs
    else:
        (o_ref,) = refs
    for p in range(y_ref.shape[0] // OUT_PART_ROWS):
        rs = slice(p * OUT_PART_ROWS, (p + 1) * OUT_PART_ROWS)
        y = jnp.dot(y_ref[rs, :], wout_ref[...], preferred_element_type=F32)
        x_new = x_ref[rs, :] + _rms_norm(y, npost_ref[...])
        o_ref[rs, :] = x_new
        if emit_next:
            h_ref[rs, :] = _rms_norm(x_new, nnext_ref[...]).astype(BF16)


def _out_call(y, wout, x, npost, npre, layer, *, rows):
    n_tok, d_model = x.shape
    emit_next = layer + 1 < npre.shape[0]
    assert n_tok % rows == 0 and rows % OUT_PART_ROWS == 0
    row_spec = pl.BlockSpec((rows, d_model), lambda i: (i, 0))
    in_specs = [
        pl.BlockSpec((rows, y.shape[-1]), lambda i: (i, 0)),
        _layer_resident(wout.shape, layer),
        row_spec,
        _layer_resident(npost.shape, layer),
    ]
    args = [y, wout, x, npost]
    out_specs = [row_spec]
    out_shape = [jax.ShapeDtypeStruct((n_tok, d_model), F32)]
    if emit_next:
        in_specs.append(_layer_resident(npre.shape, layer + 1))
        args.append(npre)
        out_specs.append(row_spec)
        out_shape.append(jax.ShapeDtypeStruct((n_tok, d_model), BF16))
    return pl.pallas_call(
        functools.partial(_out_kernel, emit_next=emit_next),
        grid=(n_tok // rows,),
        in_specs=in_specs,
        out_specs=out_specs,
        out_shape=out_shape,
        compiler_params=pltpu.CompilerParams(
            dimension_semantics=("arbitrary",),
            vmem_limit_bytes=VMEM_LIMIT_BYTES),
        name="out_proj",
    )(*args)


def _retention_tables(pos0, t_len, ns, ts, hd):
    n_t = t_len // ts
    half = hd // 2
    inv_freq = ROPE_BASE ** (-jnp.arange(half, dtype=F32) / half)
    pos = pos0 + jnp.arange(t_len)
    ang = pos.astype(F32)[:, None] * inv_freq[None, :]
    cos = jnp.cos(ang)
    sin = jnp.sin(ang)
    cos2 = jnp.concatenate([cos, cos], axis=-1).reshape(n_t, 1, ts, hd)
    sin2 = jnp.concatenate([-sin, sin], axis=-1).reshape(n_t, 1, ts, hd)
    cos_t = jnp.broadcast_to(cos2, (n_t, ns, ts, hd)).reshape(n_t, ns * ts, hd)
    sin_t = jnp.broadcast_to(sin2, (n_t, ns, ts, hd)).reshape(n_t, ns * ts, hd)

    log_g = jnp.log1p(-jnp.exp2(-5.0 - jnp.arange(N_RET_HEADS, dtype=F32)))
    i = jnp.arange(ts, dtype=F32)
    lg = log_g[:, None]
    qdec = jnp.exp((i + 1.0)[None, :] * lg)
    kdec = jnp.exp((ts - 1.0 - i)[None, :] * lg)
    rdec = jnp.exp(ts * log_g)

    def rows_table(v):
        v = jnp.broadcast_to(v[:, None, :, None], (N_RET_HEADS, ns, ts, hd))
        return v.reshape(N_RET_HEADS, ns * ts, hd)

    dist = jnp.abs(i[:, None] - i[None, :])
    dm = jnp.exp(dist[None] * log_g[:, None, None])
    chunk_id = jnp.arange(ts) // CHUNK
    causal = (chunk_id[None, :] <= chunk_id[:, None]).astype(F32)
    dm = dm * causal[None]
    seq_eye = jnp.eye(ns, dtype=F32)
    dmat = (seq_eye[None, :, None, :, None] * dm[:, None, :, None, :]).reshape(
        N_RET_HEADS, ns * ts, ns * ts)
    rdec_t = jnp.broadcast_to(rdec[:, None], (N_RET_HEADS, hd))
    return cos_t, sin_t, dmat, rows_table(qdec), rows_table(kdec), rdec_t


def _trunk(x, pos0, conv_bufs, r0s, params, *, ns, ts, out_rows, vmem_limit):
    norm_pre, mixer_weights, w_out, norm_post = params
    depth = norm_pre.shape[0]
    bsz, t_len, d_model = x.shape
    n_tok = bsz * t_len
    hd = mixer_weights[-1].shape[-1] // N_RET_HEADS
    tables = _retention_tables(pos0, t_len, ns, ts, hd)
    bufs_p = jnp.pad(conv_bufs, ((0, 0), (0, 0), (HIST_PAD, 0), (0, 0)))
    x = x.reshape(n_tok, d_model)
    h = None
    bufs, states = [], []
    for l in range(depth):
        xin = x if h is None else h
        y, nbuf, r_new = _mixer_call(
            xin.reshape(bsz, t_len, d_model), norm_pre if h is None else None, l, mixer_weights, tables,
            bufs_p, r0s, ns=ns, ts=ts, vmem_limit=vmem_limit)
        outs = _out_call(y.reshape(n_tok, -1), w_out, x, norm_post, norm_pre, l, rows=out_rows)
        x = outs[0]
        h = outs[1] if len(outs) > 1 else None
        bufs.append(nbuf[:, HIST_PAD:, :])
        states.append(r_new)
    return x.reshape(bsz, t_len, d_model), jnp.stack(bufs), jnp.stack(states)


def kernel(x_prompt, x_sample, cache_conv, state_ret, norm_pre, w_in, conv_w, conv_b, conv_ln_w,
           conv_ln_b, w_pw, ret_gn_w, w_out, norm_post):
    depth = norm_pre.shape[0]
    bsz, t_len, _ = x_prompt.shape
    c_conv = w_pw.shape[-1]
    hd = ret_gn_w.shape[-1] // N_RET_HEADS
    assert t_len >= CONV_STATE and x_sample.shape[1] >= CONV_STATE

    def rowvec(p):
        return p[:, None, :]

    mixer_weights = (w_in.astype(BF16), conv_w, rowvec(conv_b), rowvec(conv_ln_w), rowvec(conv_ln_b),
                     w_pw.astype(BF16), rowvec(ret_gn_w))
    params = (rowvec(norm_pre), mixer_weights, w_out.astype(BF16), rowvec(norm_post))

    zero_bufs = jnp.zeros((depth, bsz, CONV_STATE, c_conv), F32)
    zero_r = jnp.zeros((depth, bsz, N_RET_HEADS, hd, hd), F32)
    y_p, conv_p, ret_p = _trunk(x_prompt, 0, zero_bufs, zero_r, params, ns=1, ts=min(256, t_len),
                                out_rows=min(512, bsz * t_len), vmem_limit=VMEM_LIMIT_BYTES)
    sb, st, _ = x_sample.shape
    y_s, conv_s, ret_s = _trunk(x_sample, PAST_LEN, cache_conv, state_ret.astype(F32), params,
                                ns=min(4, sb), ts=st, out_rows=min(512, sb * st),
                                vmem_limit=SAMPLE_VMEM_LIMIT_BYTES)
    return (y_p, y_s, conv_p, ret_p, conv_s, ret_s)
```

```python
import functools

import jax
import jax.numpy as jnp
from jax import lax
from jax.experimental import pallas as pl
from jax.experimental.pallas import tpu as pltpu

CHUNK = 64
CONV_WIDTH = 31
CONV_STATE = CONV_WIDTH - 1
N_RET_HEADS = 8
ROPE_BASE = 10000.0
EPS = 1e-6

LANES = 128
SUBLANES = 8
HIST_ROWS = 32
HIST_PAD = HIST_ROWS - CONV_STATE
CONV_ROW_STRIDE = 4
PAST_LEN = 2048
GLU_CHUNK = 256
OUT_PART_ROWS = 256
VMEM_LIMIT_BYTES = 56 * 1024 * 1024
BIG_VMEM_LIMIT_BYTES = 60 * 1024 * 1024

F32 = jnp.float32
BF16 = jnp.bfloat16


def _rms_norm(xf, w):
    y = xf * lax.rsqrt(jnp.mean(xf * xf, axis=-1, keepdims=True) + EPS)
    return y * w


def _silu(x):
    return x * jax.nn.sigmoid(x)


def _resident(shape):
    nd = len(shape)
    return pl.BlockSpec(shape, lambda *_: (0,) * nd, pipeline_mode=pl.Buffered(1))


def _layer_resident(stacked_shape, layer):
    nd = len(stacked_shape) - 1
    return pl.BlockSpec((None,) + tuple(stacked_shape[1:]), lambda *_: (layer,) + (0,) * nd,
                        pipeline_mode=pl.Buffered(1))


def _mixer_kernel(in_ref, *refs, ns, ts, normalize):
    if normalize:
        npre_ref, *refs = refs
    (win_ref, cw_ref, cb_ref, lnw_ref, lnb_ref, wpw_ref, gnw_ref, cos_ref, sin_ref, dmat_ref, qdec_ref,
     kdec_ref, rdec_ref, buf0_ref, r0_ref, y_ref, nbuf_ref, rout_ref, ubuf_ref, c_ref, act_ref) = refs
    d_model = in_ref.shape[-1]
    c_conv = wpw_ref.shape[0]
    c_ret = gnw_ref.shape[-1]
    n_slab = c_conv // LANES
    hd = c_ret // N_RET_HEADS
    rows = ns * ts
    scale = hd ** -0.5
    ret0 = 3 * c_conv

    @pl.when(pl.program_id(1) == 0)
    def _():
        rout_ref[...] = r0_ref[...]
        for s in range(ns):
            for cc in range(n_slab):
                ubuf_ref[s, cc, 0:HIST_ROWS, :] = buf0_ref[s, :, cc * LANES:(cc + 1) * LANES]

    h = in_ref[...].reshape(rows, d_model)
    if normalize:
        h = _rms_norm(h, npre_ref[...]).astype(BF16)

    def proj(col0, width):
        return jnp.dot(h, win_ref[:, col0:col0 + width], preferred_element_type=F32)

    for c2 in range(c_conv // GLU_CHUNK):
        a = proj(c2 * GLU_CHUNK, GLU_CHUNK)
        b = proj(c_conv + c2 * GLU_CHUNK, GLU_CHUNK)
        u = a * jax.nn.sigmoid(b)
        for s in range(ns):
            for k in range(GLU_CHUNK // LANES):
                ubuf_ref[s, c2 * (GLU_CHUNK // LANES) + k, HIST_ROWS:, :] = (
                    u[s * ts:(s + 1) * ts, k * LANES:(k + 1) * LANES])
    zq = proj(ret0, c_ret)
    zk = proj(ret0 + c_ret, c_ret)
    zv = proj(ret0 + 2 * c_ret, c_ret)

    blk = CONV_ROW_STRIDE * SUBLANES
    for s in range(ns):
        for cc in range(n_slab):
            ch = slice(cc * LANES, (cc + 1) * LANES)
            bias = jnp.broadcast_to(cb_ref[:, ch], (SUBLANES, LANES))
            for rb in range(ts // blk):
                r0 = rb * blk
                accs = [bias] * CONV_ROW_STRIDE
                for j in range(CONV_WIDTH):
                    w = cw_ref[j:j + 1, ch]
                    for m in range(CONV_ROW_STRIDE):
                        tap = ubuf_ref[s, cc, pl.ds(r0 + m + j + HIST_PAD, SUBLANES,
                                                    stride=CONV_ROW_STRIDE), :]
                        accs[m] = accs[m] + w * tap
                for m in range(CONV_ROW_STRIDE):
                    c_ref[cc, pl.ds(s * ts + r0 + m, SUBLANES, stride=CONV_ROW_STRIDE), :] = accs[m]

    for s in range(ns):
        for cc in range(n_slab):
            tail = ubuf_ref[s, cc, ts:ts + HIST_ROWS, :]
            ubuf_ref[s, cc, 0:HIST_ROWS, :] = tail
            nbuf_ref[s, :, cc * LANES:(cc + 1) * LANES] = tail

    cs = [c_ref[cc] for cc in range(n_slab)]
    mu = jnp.sum(sum(cs[1:], cs[0]), axis=-1, keepdims=True) * (1.0 / c_conv)
    xcs = [c - mu for c in cs]
    sq = xcs[0] * xcs[0]
    for xc in xcs[1:]:
        sq = sq + xc * xc
    inv = lax.rsqrt(jnp.sum(sq, axis=-1, keepdims=True) * (1.0 / c_conv) + EPS)
    for cc in range(n_slab):
        ch = slice(cc * LANES, (cc + 1) * LANES)
        ln = xcs[cc] * inv * lnw_ref[:, ch] + lnb_ref[:, ch]
        act_ref[:, ch] = _silu(ln).astype(BF16)

    cos = cos_ref[0]
    sin = sin_ref[0]
    head_cols = [slice(hh * hd, (hh + 1) * hd) for hh in range(N_RET_HEADS)]
    qs, ks, vbs, scores = [], [], [], []
    for sl in head_cols:
        q = zq[:, sl]
        k = zk[:, sl]
        q = q * cos + pltpu.roll(q, hd // 2, 1) * sin
        k = (k * cos + pltpu.roll(k, hd // 2, 1) * sin) * scale
        qs.append(q)
        ks.append(k)
        vbs.append(zv[:, sl].astype(BF16))
        scores.append(lax.dot_general(q.astype(BF16), k.astype(BF16), (((1,), (1,)), ((), ())),
                                      preferred_element_type=F32))
    inters = []
    for hh in range(N_RET_HEADS):
        qd = (qs[hh] * qdec_ref[hh]).astype(BF16)
        kd = (ks[hh] * kdec_ref[hh]).astype(BF16)
        inter = []
        for s in range(ns):
            rs = slice(s * ts, (s + 1) * ts)
            state = rout_ref[s, hh]
            inter.append(jnp.dot(qd[rs], state.astype(BF16), preferred_element_type=F32))
            kv = lax.dot_general(kd[rs], vbs[hh][rs], (((0,), (0,)), ((), ())),
                                 preferred_element_type=F32)
            rout_ref[s, hh] = state * rdec_ref[hh:hh + 1, :] + kv
        inters.append(inter[0] if ns == 1 else jnp.concatenate(inter, axis=0))
    normed = []
    for hh in range(N_RET_HEADS):
        sc = (scores[hh] * dmat_ref[hh]).astype(BF16)
        o = jnp.dot(sc, vbs[hh], preferred_element_type=F32) + inters[hh]
        mu = jnp.mean(o, axis=-1, keepdims=True)
        oc = o - mu
        on = oc * lax.rsqrt(jnp.mean(oc * oc, axis=-1, keepdims=True) + EPS)
        normed.append(on * gnw_ref[:, head_cols[hh]])

    zg = proj(ret0 + 3 * c_ret, c_ret)
    g_conv = proj(2 * c_conv, c_conv)
    for hh in range(N_RET_HEADS):
        y_ref[:, :, c_conv + hh * hd:c_conv + (hh + 1) * hd] = (
            (_silu(zg[:, hh * hd:(hh + 1) * hd]) * normed[hh]).astype(BF16).reshape(ns, ts, hd))
    cp = jnp.dot(act_ref[...], wpw_ref[...], preferred_element_type=F32)
    y_ref[:, :, 0:c_conv] = (_silu(g_conv) * cp).astype(BF16).reshape(ns, ts, c_conv)


def _mixer_call(xin, npre, layer, weights, tables, bufs_p, r0s, *, ns, ts, vmem_limit):
    win, cw, cb, lnw, lnb, wpw, gnw = weights
    cos_t, sin_t, dmat, qdec, kdec, rdec = tables
    bsz, t_len, d_model = xin.shape
    c_conv = wpw.shape[-1]
    c_ret = gnw.shape[-1]
    hd = c_ret // N_RET_HEADS
    rows = ns * ts
    normalize = npre is not None
    assert ts % CHUNK == 0 and ts % (CONV_ROW_STRIDE * SUBLANES) == 0 and ts >= HIST_ROWS
    assert bsz % ns == 0 and t_len % ts == 0 and c_conv % GLU_CHUNK == 0
    kern = functools.partial(_mixer_kernel, ns=ns, ts=ts, normalize=normalize)
    state_in = pl.BlockSpec((None, ns, N_RET_HEADS, hd, hd), lambda b, t: (layer, b, 0, 0, 0),
                            pipeline_mode=pl.Buffered(1))
    hist_in = pl.BlockSpec((None, ns, HIST_ROWS, c_conv), lambda b, t: (layer, b, 0, 0),
                           pipeline_mode=pl.Buffered(1))
    in_specs = [pl.BlockSpec((ns, ts, d_model), lambda b, t: (b, t, 0))]
    args = [xin]
    if normalize:
        in_specs.append(_layer_resident(npre.shape, layer))
        args.append(npre)
    in_specs += [_layer_resident(w.shape, layer) for w in weights]
    in_specs += [
        pl.BlockSpec((1, rows, hd), lambda b, t: (t, 0, 0)),
        pl.BlockSpec((1, rows, hd), lambda b, t: (t, 0, 0)),
        _resident(dmat.shape),
        _resident(qdec.shape),
        _resident(kdec.shape),
        _resident(rdec.shape),
        hist_in,
        state_in,
    ]
    args += [*weights, cos_t, sin_t, dmat, qdec, kdec, rdec, bufs_p, r0s]
    return pl.pallas_call(
        kern,
        grid=(bsz // ns, t_len // ts),
        in_specs=in_specs,
        out_specs=[
            pl.BlockSpec((ns, ts, c_conv + c_ret), lambda b, t: (b, t, 0)),
            pl.BlockSpec((ns, HIST_ROWS, c_conv), lambda b, t: (b, 0, 0)),
            pl.BlockSpec((ns, N_RET_HEADS, hd, hd), lambda b, t: (b, 0, 0, 0)),
        ],
        out_shape=[
            jax.ShapeDtypeStruct((bsz, t_len, c_conv + c_ret), BF16),
            jax.ShapeDtypeStruct((bsz, HIST_ROWS, c_conv), F32),
            jax.ShapeDtypeStruct((bsz, N_RET_HEADS, hd, hd), F32),
        ],
        scratch_shapes=[
            pltpu.VMEM((ns, c_conv // LANES, HIST_ROWS + ts, LANES), F32),
            pltpu.VMEM((c_conv // LANES, rows, LANES), F32),
            pltpu.VMEM((rows, c_conv), BF16),
        ],
        compiler_params=pltpu.CompilerParams(
            dimension_semantics=("arbitrary", "arbitrary"),
            vmem_limit_bytes=vmem_limit),
        name="mixer",
    )(*args)


def _out_kernel(y_ref, wout_ref, x_ref, npost_ref, *refs, emit_next):
    if emit_next:
        nnext_ref, o_ref, h_ref = refs
    else:
        (o_ref,) = refs
    for p in range(y_ref.shape[0] // OUT_PART_ROWS):
        rs = slice(p * OUT_PART_ROWS, (p + 1) * OUT_PART_ROWS)
        y = jnp.dot(y_ref[rs, :], wout_ref[...], preferred_element_type=F32)
        x_new = x_ref[rs, :] + _rms_norm(y, npost_ref[...])
        o_ref[rs, :] = x_new
        if emit_next:
            h_ref[rs, :] = _rms_norm(x_new, nnext_ref[...]).astype(BF16)


def _out_call(y, wout, x, npost, npre, layer, *, rows):
    n_tok, d_model = x.shape
    emit_next = layer + 1 < npre.shape[0]
    assert n_tok % rows == 0 and rows % OUT_PART_ROWS == 0
    row_spec = pl.BlockSpec((rows, d_model), lambda i: (i, 0))
    in_specs = [
        pl.BlockSpec((rows, y.shape[-1]), lambda i: (i, 0)),
        _layer_resident(wout.shape, layer),
        row_spec,
        _layer_resident(npost.shape, layer),
    ]
    args = [y, wout, x, npost]
    out_specs = [row_spec]
    out_shape = [jax.ShapeDtypeStruct((n_tok, d_model), F32)]
    if emit_next:
        in_specs.append(_layer_resident(npre.shape, layer + 1))
        args.append(npre)
        out_specs.append(row_spec)
        out_shape.append(jax.ShapeDtypeStruct((n_tok, d_model), BF16))
    return pl.pallas_call(
        functools.partial(_out_kernel, emit_next=emit_next),
        grid=(n_tok // rows,),
        in_specs=in_specs,
        out_specs=out_specs,
        out_shape=out_shape,
        compiler_params=pltpu.CompilerParams(
            dimension_semantics=("arbitrary",),
            vmem_limit_bytes=BIG_VMEM_LIMIT_BYTES),
        name="out_proj",
    )(*args)


def _retention_tables(pos0, t_len, ns, ts, hd):
    n_t = t_len // ts
    half = hd // 2
    inv_freq = ROPE_BASE ** (-jnp.arange(half, dtype=F32) / half)
    pos = pos0 + jnp.arange(t_len)
    ang = pos.astype(F32)[:, None] * inv_freq[None, :]
    cos = jnp.cos(ang)
    sin = jnp.sin(ang)
    cos2 = jnp.concatenate([cos, cos], axis=-1).reshape(n_t, 1, ts, hd)
    sin2 = jnp.concatenate([-sin, sin], axis=-1).reshape(n_t, 1, ts, hd)
    cos_t = jnp.broadcast_to(cos2, (n_t, ns, ts, hd)).reshape(n_t, ns * ts, hd)
    sin_t = jnp.broadcast_to(sin2, (n_t, ns, ts, hd)).reshape(n_t, ns * ts, hd)

    log_g = jnp.log1p(-jnp.exp2(-5.0 - jnp.arange(N_RET_HEADS, dtype=F32)))
    i = jnp.arange(ts, dtype=F32)
    lg = log_g[:, None]
    qdec = jnp.exp((i + 1.0)[None, :] * lg)
    kdec = jnp.exp((ts - 1.0 - i)[None, :] * lg)
    rdec = jnp.exp(ts * log_g)

    def rows_table(v):
        v = jnp.broadcast_to(v[:, None, :, None], (N_RET_HEADS, ns, ts, hd))
        return v.reshape(N_RET_HEADS, ns * ts, hd)

    dist = jnp.abs(i[:, None] - i[None, :])
    dm = jnp.exp(dist[None] * log_g[:, None, None])
    chunk_id = jnp.arange(ts) // CHUNK
    causal = (chunk_id[None, :] <= chunk_id[:, None]).astype(F32)
    dm = dm * causal[None]
    seq_eye = jnp.eye(ns, dtype=F32)
    dmat = (seq_eye[None, :, None, :, None] * dm[:, None, :, None, :]).reshape(
        N_RET_HEADS, ns * ts, ns * ts)
    rdec_t = jnp.broadcast_to(rdec[:, None], (N_RET_HEADS, hd))
    return cos_t, sin_t, dmat, rows_table(qdec), rows_table(kdec), rdec_t


def _trunk(x, pos0, conv_bufs, r0s, params, *, ns, ts, out_rows, vmem_limit):
    norm_pre, mixer_weights, w_out, norm_post = params
    depth = norm_pre.shape[0]
    bsz, t_len, d_model = x.shape
    n_tok = bsz * t_len
    hd = mixer_weights[-1].shape[-1] // N_RET_HEADS
    tables = _retention_tables(pos0, t_len, ns, ts, hd)
    bufs_p = jnp.pad(conv_bufs, ((0, 0), (0, 0), (HIST_PAD, 0), (0, 0)))
    x = x.reshape(n_tok, d_model)
    h = None
    bufs, states = [], []
    for l in range(depth):
        xin = x if h is None else h
        y, nbuf, r_new = _mixer_call(
            xin.reshape(bsz, t_len, d_model), norm_pre if h is None else None, l, mixer_weights, tables,
            bufs_p, r0s, ns=ns, ts=ts, vmem_limit=vmem_limit)
        outs = _out_call(y.reshape(n_tok, -1), w_out, x, norm_post, norm_pre, l, rows=out_rows)
        x = outs[0]
        h = outs[1] if len(outs) > 1 else None
        bufs.append(nbuf[:, HIST_PAD:, :])
        states.append(r_new)
    return x.reshape(bsz, t_len, d_model), jnp.stack(bufs), jnp.stack(states)


def kernel(x_prompt, x_sample, cache_conv, state_ret, norm_pre, w_in, conv_w, conv_b, conv_ln_w,
           conv_ln_b, w_pw, ret_gn_w, w_out, norm_post):
    depth = norm_pre.shape[0]
    bsz, t_len, _ = x_prompt.shape
    c_conv = w_pw.shape[-1]
    hd = ret_gn_w.shape[-1] // N_RET_HEADS
    assert t_len >= CONV_STATE and x_sample.shape[1] >= CONV_STATE

    def rowvec(p):
        return p[:, None, :]

    mixer_weights = (w_in.astype(BF16), conv_w, rowvec(conv_b), rowvec(conv_ln_w), rowvec(conv_ln_b),
                     w_pw.astype(BF16), rowvec(ret_gn_w))
    params = (rowvec(norm_pre), mixer_weights, w_out.astype(BF16), rowvec(norm_post))

    zero_bufs = jnp.zeros((depth, bsz, CONV_STATE, c_conv), F32)
    zero_r = jnp.zeros((depth, bsz, N_RET_HEADS, hd, hd), F32)
    y_p, conv_p, ret_p = _trunk(x_prompt, 0, zero_bufs, zero_r, params, ns=1, ts=min(256, t_len),
                                out_rows=min(1024, bsz * t_len), vmem_limit=VMEM_LIMIT_BYTES)
    sb, st, _ = x_sample.shape
    y_s, conv_s, ret_s = _trunk(x_sample, PAST_LEN, cache_conv, state_ret.astype(F32), params,
                                ns=min(4, sb), ts=st, out_rows=min(1024, sb * st),
                                vmem_limit=BIG_VMEM_LIMIT_BYTES)
    return (y_p, y_s, conv_p, ret_p, conv_s, ret_s)
```

```python
import functools

import jax
import jax.numpy as jnp
from jax import lax
from jax.experimental import pallas as pl
from jax.experimental.pallas import tpu as pltpu

CHUNK = 64
CONV_WIDTH = 31
CONV_STATE = CONV_WIDTH - 1
N_RET_HEADS = 8
ROPE_BASE = 10000.0
EPS = 1e-6

LANES = 128
SUBLANES = 8
HIST_ROWS = 32
HIST_PAD = HIST_ROWS - CONV_STATE
CONV_ROW_STRIDE = 4
PAST_LEN = 2048
GLU_CHUNK = 256
OUT_IN_BUFFERS = 3
OUT_PART_ROWS = 256
VMEM_LIMIT_BYTES = 56 * 1024 * 1024
BIG_VMEM_LIMIT_BYTES = 60 * 1024 * 1024

F32 = jnp.float32
BF16 = jnp.bfloat16


def _rms_norm(xf, w):
    y = xf * lax.rsqrt(jnp.mean(xf * xf, axis=-1, keepdims=True) + EPS)
    return y * w


def _silu(x):
    return x * jax.nn.sigmoid(x)


def _resident(shape):
    nd = len(shape)
    return pl.BlockSpec(shape, lambda *_: (0,) * nd, pipeline_mode=pl.Buffered(1))


def _layer_resident(stacked_shape, layer):
    nd = len(stacked_shape) - 1
    return pl.BlockSpec((None,) + tuple(stacked_shape[1:]), lambda *_: (layer,) + (0,) * nd,
                        pipeline_mode=pl.Buffered(1))


def _mixer_kernel(in_ref, *refs, ns, ts, normalize):
    if normalize:
        npre_ref, *refs = refs
    (win_ref, cw_ref, cb_ref, lnw_ref, lnb_ref, wpw_ref, gnw_ref, cos_ref, sin_ref, dmat_ref, qdec_ref,
     kdec_ref, rdec_ref, buf0_ref, r0_ref, y_ref, nbuf_ref, rout_ref, ubuf_ref, c_ref, act_ref) = refs
    d_model = in_ref.shape[-1]
    c_conv = wpw_ref.shape[0]
    c_ret = gnw_ref.shape[-1]
    n_slab = c_conv // LANES
    hd = c_ret // N_RET_HEADS
    rows = ns * ts
    scale = hd ** -0.5
    ret0 = 3 * c_conv

    @pl.when(pl.program_id(1) == 0)
    def _():
        rout_ref[...] = r0_ref[...]
        for s in range(ns):
            for cc in range(n_slab):
                ubuf_ref[s, cc, 0:HIST_ROWS, :] = buf0_ref[s, :, cc * LANES:(cc + 1) * LANES]

    h = in_ref[...].reshape(rows, d_model)
    if normalize:
        h = _rms_norm(h, npre_ref[...]).astype(BF16)

    def proj(col0, width):
        return jnp.dot(h, win_ref[:, col0:col0 + width], preferred_element_type=F32)

    for c2 in range(c_conv // GLU_CHUNK):
        a = proj(c2 * GLU_CHUNK, GLU_CHUNK)
        b = proj(c_conv + c2 * GLU_CHUNK, GLU_CHUNK)
        u = a * jax.nn.sigmoid(b)
        for s in range(ns):
            for k in range(GLU_CHUNK // LANES):
                ubuf_ref[s, c2 * (GLU_CHUNK // LANES) + k, HIST_ROWS:, :] = (
                    u[s * ts:(s + 1) * ts, k * LANES:(k + 1) * LANES])
    zq = proj(ret0, c_ret)
    zk = proj(ret0 + c_ret, c_ret)
    zv = proj(ret0 + 2 * c_ret, c_ret)

    blk = CONV_ROW_STRIDE * SUBLANES
    for s in range(ns):
        for cc in range(n_slab):
            ch = slice(cc * LANES, (cc + 1) * LANES)
            bias = jnp.broadcast_to(cb_ref[:, ch], (SUBLANES, LANES))
            for rb in range(ts // blk):
                r0 = rb * blk
                accs = [bias] * CONV_ROW_STRIDE
                for j in range(CONV_WIDTH):
                    w = cw_ref[j:j + 1, ch]
                    for m in range(CONV_ROW_STRIDE):
                        tap = ubuf_ref[s, cc, pl.ds(r0 + m + j + HIST_PAD, SUBLANES,
                                                    stride=CONV_ROW_STRIDE), :]
                        accs[m] = accs[m] + w * tap
                for m in range(CONV_ROW_STRIDE):
                    c_ref[cc, pl.ds(s * ts + r0 + m, SUBLANES, stride=CONV_ROW_STRIDE), :] = accs[m]

    for s in range(ns):
        for cc in range(n_slab):
            tail = ubuf_ref[s, cc, ts:ts + HIST_ROWS, :]
            ubuf_ref[s, cc, 0:HIST_ROWS, :] = tail
            nbuf_ref[s, :, cc * LANES:(cc + 1) * LANES] = tail

    cs = [c_ref[cc] for cc in range(n_slab)]
    mu = jnp.sum(sum(cs[1:], cs[0]), axis=-1, keepdims=True) * (1.0 / c_conv)
    xcs = [c - mu for c in cs]
    sq = xcs[0] * xcs[0]
    for xc in xcs[1:]:
        sq = sq + xc * xc
    inv = lax.rsqrt(jnp.sum(sq, axis=-1, keepdims=True) * (1.0 / c_conv) + EPS)
    for cc in range(n_slab):
        ch = slice(cc * LANES, (cc + 1) * LANES)
        ln = xcs[cc] * inv * lnw_ref[:, ch] + lnb_ref[:, ch]
        act_ref[:, ch] = _silu(ln).astype(BF16)

    cos = cos_ref[0]
    sin = sin_ref[0]
    head_cols = [slice(hh * hd, (hh + 1) * hd) for hh in range(N_RET_HEADS)]
    qs, ks, vbs, scores = [], [], [], []
    for sl in head_cols:
        q = zq[:, sl]
        k = zk[:, sl]
        q = q * cos + pltpu.roll(q, hd // 2, 1) * sin
        k = (k * cos + pltpu.roll(k, hd // 2, 1) * sin) * scale
        qs.append(q)
        ks.append(k)
        vbs.append(zv[:, sl].astype(BF16))
        scores.append(lax.dot_general(q.astype(BF16), k.astype(BF16), (((1,), (1,)), ((), ())),
                                      preferred_element_type=F32))
    inters = []
    for hh in range(N_RET_HEADS):
        qd = (qs[hh] * qdec_ref[hh]).astype(BF16)
        kd = (ks[hh] * kdec_ref[hh]).astype(BF16)
        inter = []
        for s in range(ns):
            rs = slice(s * ts, (s + 1) * ts)
            state = rout_ref[s, hh]
            inter.append(jnp.dot(qd[rs], state.astype(BF16), preferred_element_type=F32))
            kv = lax.dot_general(kd[rs], vbs[hh][rs], (((0,), (0,)), ((), ())),
                                 preferred_element_type=F32)
            rout_ref[s, hh] = state * rdec_ref[hh:hh + 1, :] + kv
        inters.append(inter[0] if ns == 1 else jnp.concatenate(inter, axis=0))
    normed = []
    for hh in range(N_RET_HEADS):
        sc = (scores[hh] * dmat_ref[hh]).astype(BF16)
        o = jnp.dot(sc, vbs[hh], preferred_element_type=F32) + inters[hh]
        mu = jnp.mean(o, axis=-1, keepdims=True)
        oc = o - mu
        on = oc * lax.rsqrt(jnp.mean(oc * oc, axis=-1, keepdims=True) + EPS)
        normed.append(on * gnw_ref[:, head_cols[hh]])

    zg = proj(ret0 + 3 * c_ret, c_ret)
    g_conv = proj(2 * c_conv, c_conv)
    for hh in range(N_RET_HEADS):
        y_ref[:, :, c_conv + hh * hd:c_conv + (hh + 1) * hd] = (
            (_silu(zg[:, hh * hd:(hh + 1) * hd]) * normed[hh]).astype(BF16).reshape(ns, ts, hd))
    cp = jnp.dot(act_ref[...], wpw_ref[...], preferred_element_type=F32)
    y_ref[:, :, 0:c_conv] = (_silu(g_conv) * cp).astype(BF16).reshape(ns, ts, c_conv)


def _mixer_call(xin, npre, layer, weights, tables, bufs_p, r0s, *, ns, ts, vmem_limit):
    win, cw, cb, lnw, lnb, wpw, gnw = weights
    cos_t, sin_t, dmat, qdec, kdec, rdec = tables
    bsz, t_len, d_model = xin.shape
    c_conv = wpw.shape[-1]
    c_ret = gnw.shape[-1]
    hd = c_ret // N_RET_HEADS
    rows = ns * ts
    normalize = npre is not None
    assert ts % CHUNK == 0 and ts % (CONV_ROW_STRIDE * SUBLANES) == 0 and ts >= HIST_ROWS
    assert bsz % ns == 0 and t_len % ts == 0 and c_conv % GLU_CHUNK == 0
    kern = functools.partial(_mixer_kernel, ns=ns, ts=ts, normalize=normalize)
    state_in = pl.BlockSpec((None, ns, N_RET_HEADS, hd, hd), lambda b, t: (layer, b, 0, 0, 0),
                            pipeline_mode=pl.Buffered(1))
    hist_in = pl.BlockSpec((None, ns, HIST_ROWS, c_conv), lambda b, t: (layer, b, 0, 0),
                           pipeline_mode=pl.Buffered(1))
    in_specs = [pl.BlockSpec((ns, ts, d_model), lambda b, t: (b, t, 0))]
    args = [xin]
    if normalize:
        in_specs.append(_layer_resident(npre.shape, layer))
        args.append(npre)
    in_specs += [_layer_resident(w.shape, layer) for w in weights]
    in_specs += [
        pl.BlockSpec((1, rows, hd), lambda b, t: (t, 0, 0)),
        pl.BlockSpec((1, rows, hd), lambda b, t: (t, 0, 0)),
        _resident(dmat.shape),
        _resident(qdec.shape),
        _resident(kdec.shape),
        _resident(rdec.shape),
        hist_in,
        state_in,
    ]
    args += [*weights, cos_t, sin_t, dmat, qdec, kdec, rdec, bufs_p, r0s]
    return pl.pallas_call(
        kern,
        grid=(bsz // ns, t_len // ts),
        in_specs=in_specs,
        out_specs=[
            pl.BlockSpec((ns, ts, c_conv + c_ret), lambda b, t: (b, t, 0)),
            pl.BlockSpec((ns, HIST_ROWS, c_conv), lambda b, t: (b, 0, 0)),
            pl.BlockSpec((ns, N_RET_HEADS, hd, hd), lambda b, t: (b, 0, 0, 0)),
        ],
        out_shape=[
            jax.ShapeDtypeStruct((bsz, t_len, c_conv + c_ret), BF16),
            jax.ShapeDtypeStruct((bsz, HIST_ROWS, c_conv), F32),
            jax.ShapeDtypeStruct((bsz, N_RET_HEADS, hd, hd), F32),
        ],
        scratch_shapes=[
            pltpu.VMEM((ns, c_conv // LANES, HIST_ROWS + ts, LANES), F32),
            pltpu.VMEM((c_conv // LANES, rows, LANES), F32),
            pltpu.VMEM((rows, c_conv), BF16),
        ],
        compiler_params=pltpu.CompilerParams(
            dimension_semantics=("arbitrary", "arbitrary"),
            vmem_limit_bytes=vmem_limit),
        name="mixer",
    )(*args)


def _out_kernel(y_hbm, wout_ref, x_hbm, npost_ref, *refs, emit_next, rows):
    if emit_next:
        nnext_ref, o_hbm, h_hbm = refs
    else:
        (o_hbm,) = refs
    n_tok, d_model = x_hbm.shape

    def tile_body(y_ref, x_ref, o_ref, *maybe_h_ref):
        for p in range(rows // OUT_PART_ROWS):
            rs = slice(p * OUT_PART_ROWS, (p + 1) * OUT_PART_ROWS)
            y = jnp.dot(y_ref[rs, :], wout_ref[...], preferred_element_type=F32)
            x_new = x_ref[rs, :] + _rms_norm(y, npost_ref[...])
            o_ref[rs, :] = x_new
            if emit_next:
                maybe_h_ref[0][rs, :] = _rms_norm(x_new, nnext_ref[...]).astype(BF16)

    def in_spec(width):
        return pl.BlockSpec((rows, width), lambda i: (i, 0), pipeline_mode=pl.Buffered(OUT_IN_BUFFERS))

    row_spec = pl.BlockSpec((rows, d_model), lambda i: (i, 0))
    outs = [o_hbm] + ([h_hbm] if emit_next else [])
    pltpu.emit_pipeline(
        tile_body,
        grid=(n_tok // rows,),
        in_specs=[in_spec(y_hbm.shape[-1]), in_spec(d_model)],
        out_specs=[row_spec] * len(outs),
    )(y_hbm, x_hbm, *outs)


def _out_call(y, wout, x, npost, npre, layer, *, rows):
    n_tok, d_model = x.shape
    emit_next = layer + 1 < npre.shape[0]
    assert n_tok % rows == 0 and rows % OUT_PART_ROWS == 0

    def whole_layer(stacked, l):
        nd = stacked.ndim - 1
        return pl.BlockSpec((None,) + stacked.shape[1:], lambda _: (l,) + (0,) * nd)

    hbm = pl.BlockSpec(memory_space=pl.ANY)
    in_specs = [hbm, whole_layer(wout, layer), hbm, whole_layer(npost, layer)]
    args = [y, wout, x, npost]
    out_specs = [hbm]
    out_shape = [jax.ShapeDtypeStruct((n_tok, d_model), F32)]
    if emit_next:
        in_specs.append(whole_layer(npre, layer + 1))
        args.append(npre)
        out_specs.append(hbm)
        out_shape.append(jax.ShapeDtypeStruct((n_tok, d_model), BF16))
    return pl.pallas_call(
        functools.partial(_out_kernel, emit_next=emit_next, rows=rows),
        grid=(1,),
        in_specs=in_specs,
        out_specs=out_specs,
        out_shape=out_shape,
        compiler_params=pltpu.CompilerParams(vmem_limit_bytes=BIG_VMEM_LIMIT_BYTES),
        name="out_proj",
    )(*args)


def _retention_tables(pos0, t_len, ns, ts, hd):
    n_t = t_len // ts
    half = hd // 2
    inv_freq = ROPE_BASE ** (-jnp.arange(half, dtype=F32) / half)
    pos = pos0 + jnp.arange(t_len)
    ang = pos.astype(F32)[:, None] * inv_freq[None, :]
    cos = jnp.cos(ang)
    sin = jnp.sin(ang)
    cos2 = jnp.concatenate([cos, cos], axis=-1).reshape(n_t, 1, ts, hd)
    sin2 = jnp.concatenate([-sin, sin], axis=-1).reshape(n_t, 1, ts, hd)
    cos_t = jnp.broadcast_to(cos2, (n_t, ns, ts, hd)).reshape(n_t, ns * ts, hd)
    sin_t = jnp.broadcast_to(sin2, (n_t, ns, ts, hd)).reshape(n_t, ns * ts, hd)

    log_g = jnp.log1p(-jnp.exp2(-5.0 - jnp.arange(N_RET_HEADS, dtype=F32)))
    i = jnp.arange(ts, dtype=F32)
    lg = log_g[:, None]
    qdec = jnp.exp((i + 1.0)[None, :] * lg)
    kdec = jnp.exp((ts - 1.0 - i)[None, :] * lg)
    rdec = jnp.exp(ts * log_g)

    def rows_table(v):
        v = jnp.broadcast_to(v[:, None, :, None], (N_RET_HEADS, ns, ts, hd))
        return v.reshape(N_RET_HEADS, ns * ts, hd)

    dist = jnp.abs(i[:, None] - i[None, :])
    dm = jnp.exp(dist[None] * log_g[:, None, None])
    chunk_id = jnp.arange(ts) // CHUNK
    causal = (chunk_id[None, :] <= chunk_id[:, None]).astype(F32)
    dm = dm * causal[None]
    seq_eye = jnp.eye(ns, dtype=F32)
    dmat = (seq_eye[None, :, None, :, None] * dm[:, None, :, None, :]).reshape(
        N_RET_HEADS, ns * ts, ns * ts)
    rdec_t = jnp.broadcast_to(rdec[:, None], (N_RET_HEADS, hd))
    return cos_t, sin_t, dmat, rows_table(qdec), rows_table(kdec), rdec_t


def _trunk(x, pos0, conv_bufs, r0s, params, *, ns, ts, out_rows, vmem_limit):
    norm_pre, mixer_weights, w_out, norm_post = params
    depth = norm_pre.shape[0]
    bsz, t_len, d_model = x.shape
    n_tok = bsz * t_len
    hd = mixer_weights[-1].shape[-1] // N_RET_HEADS
    tables = _retention_tables(pos0, t_len, ns, ts, hd)
    bufs_p = jnp.pad(conv_bufs, ((0, 0), (0, 0), (HIST_PAD, 0), (0, 0)))
    x = x.reshape(n_tok, d_model)
    h = None
    bufs, states = [], []
    for l in range(depth):
        xin = x if h is None else h
        y, nbuf, r_new = _mixer_call(
            xin.reshape(bsz, t_len, d_model), norm_pre if h is None else None, l, mixer_weights, tables,
            bufs_p, r0s, ns=ns, ts=ts, vmem_limit=vmem_limit)
        outs = _out_call(y.reshape(n_tok, -1), w_out, x, norm_post, norm_pre, l, rows=out_rows)
        x = outs[0]
        h = outs[1] if len(outs) > 1 else None
        bufs.append(nbuf[:, HIST_PAD:, :])
        states.append(r_new)
    return x.reshape(bsz, t_len, d_model), jnp.stack(bufs), jnp.stack(states)


def kernel(x_prompt, x_sample, cache_conv, state_ret, norm_pre, w_in, conv_w, conv_b, conv_ln_w,
           conv_ln_b, w_pw, ret_gn_w, w_out, norm_post):
    depth = norm_pre.shape[0]
    bsz, t_len, _ = x_prompt.shape
    c_conv = w_pw.shape[-1]
    hd = ret_gn_w.shape[-1] // N_RET_HEADS
    assert t_len >= CONV_STATE and x_sample.shape[1] >= CONV_STATE

    def rowvec(p):
        return p[:, None, :]

    mixer_weights = (w_in.astype(BF16), conv_w, rowvec(conv_b), rowvec(conv_ln_w), rowvec(conv_ln_b),
                     w_pw.astype(BF16), rowvec(ret_gn_w))
    params = (rowvec(norm_pre), mixer_weights, w_out.astype(BF16), rowvec(norm_post))

    zero_bufs = jnp.zeros((depth, bsz, CONV_STATE, c_conv), F32)
    zero_r = jnp.zeros((depth, bsz, N_RET_HEADS, hd, hd), F32)
    y_p, conv_p, ret_p = _trunk(x_prompt, 0, zero_bufs, zero_r, params, ns=1, ts=min(256, t_len),
                                out_rows=min(512, bsz * t_len), vmem_limit=VMEM_LIMIT_BYTES)
    sb, st, _ = x_sample.shape
    y_s, conv_s, ret_s = _trunk(x_sample, PAST_LEN, cache_conv, state_ret.astype(F32), params,
                                ns=min(4, sb), ts=st, out_rows=min(512, sb * st),
                                vmem_limit=BIG_VMEM_LIMIT_BYTES)
    return (y_p, y_s, conv_p, ret_p, conv_s, ret_s)
```

```python
import functools

import jax
import jax.numpy as jnp
from jax import lax
from jax.experimental import pallas as pl
from jax.experimental.pallas import tpu as pltpu

CHUNK = 64
CONV_WIDTH = 31
CONV_STATE = CONV_WIDTH - 1
N_RET_HEADS = 8
ROPE_BASE = 10000.0
EPS = 1e-6

LANES = 128
SUBLANES = 8
HIST_ROWS = 32
HIST_PAD = HIST_ROWS - CONV_STATE
CONV_ROW_STRIDE = 4
PAST_LEN = 2048
GLU_CHUNK = 256
OUT_IN_BUFFERS = 3
OUT_PART_ROWS = 256
VMEM_LIMIT_BYTES = 56 * 1024 * 1024
BIG_VMEM_LIMIT_BYTES = 60 * 1024 * 1024

F32 = jnp.float32
BF16 = jnp.bfloat16


def _rms_norm(xf, w):
    y = xf * lax.rsqrt(jnp.mean(xf * xf, axis=-1, keepdims=True) + EPS)
    return y * w


def _silu(x):
    return x * jax.nn.sigmoid(x)


def _resident(shape):
    nd = len(shape)
    return pl.BlockSpec(shape, lambda *_: (0,) * nd, pipeline_mode=pl.Buffered(1))


def _layer_resident(stacked_shape, layer):
    nd = len(stacked_shape) - 1
    return pl.BlockSpec((None,) + tuple(stacked_shape[1:]), lambda *_: (layer,) + (0,) * nd,
                        pipeline_mode=pl.Buffered(1))


def _mixer_kernel(in_ref, *refs, ns, ts, normalize):
    if normalize:
        npre_ref, *refs = refs
    (win_ref, cw_ref, cb_ref, lnw_ref, lnb_ref, wpw_ref, gnw_ref, cos_ref, sin_ref, dmat_ref, qdec_ref,
     kdec_ref, rdec_ref, buf0_ref, r0_ref, y_ref, nbuf_ref, rout_ref, ubuf_ref, c_ref, act_ref) = refs
    d_model = in_ref.shape[-1]
    c_conv = wpw_ref.shape[0]
    c_ret = gnw_ref.shape[-1]
    n_slab = c_conv // LANES
    hd = c_ret // N_RET_HEADS
    rows = ns * ts
    scale = hd ** -0.5
    ret0 = 3 * c_conv

    @pl.when(pl.program_id(1) == 0)
    def _():
        rout_ref[...] = r0_ref[...]
        for s in range(ns):
            for cc in range(n_slab):
                ubuf_ref[s, cc, 0:HIST_ROWS, :] = buf0_ref[s, :, cc * LANES:(cc + 1) * LANES]

    h = in_ref[...].reshape(rows, d_model)
    if normalize:
        h = _rms_norm(h, npre_ref[...]).astype(BF16)

    def proj(col0, width):
        return jnp.dot(h, win_ref[:, col0:col0 + width], preferred_element_type=F32)

    for c2 in range(c_conv // GLU_CHUNK):
        a = proj(c2 * GLU_CHUNK, GLU_CHUNK)
        b = proj(c_conv + c2 * GLU_CHUNK, GLU_CHUNK)
        u = a * jax.nn.sigmoid(b)
        for s in range(ns):
            for k in range(GLU_CHUNK // LANES):
                ubuf_ref[s, c2 * (GLU_CHUNK // LANES) + k, HIST_ROWS:, :] = (
                    u[s * ts:(s + 1) * ts, k * LANES:(k + 1) * LANES])
    zq = proj(ret0, c_ret)
    zk = proj(ret0 + c_ret, c_ret)
    zv = proj(ret0 + 2 * c_ret, c_ret)

    blk = CONV_ROW_STRIDE * SUBLANES
    for s in range(ns):
        for cc in range(n_slab):
            ch = slice(cc * LANES, (cc + 1) * LANES)
            bias = jnp.broadcast_to(cb_ref[:, ch], (SUBLANES, LANES))
            for rb in range(ts // blk):
                r0 = rb * blk
                accs = [bias] * CONV_ROW_STRIDE
                for j in range(CONV_WIDTH):
                    w = cw_ref[j:j + 1, ch]
                    for m in range(CONV_ROW_STRIDE):
                        tap = ubuf_ref[s, cc, pl.ds(r0 + m + j + HIST_PAD, SUBLANES,
                                                    stride=CONV_ROW_STRIDE), :]
                        accs[m] = accs[m] + w * tap
                for m in range(CONV_ROW_STRIDE):
                    c_ref[cc, pl.ds(s * ts + r0 + m, SUBLANES, stride=CONV_ROW_STRIDE), :] = accs[m]

    for s in range(ns):
        for cc in range(n_slab):
            tail = ubuf_ref[s, cc, ts:ts + HIST_ROWS, :]
            ubuf_ref[s, cc, 0:HIST_ROWS, :] = tail
            nbuf_ref[s, :, cc * LANES:(cc + 1) * LANES] = tail

    half_rows = rows // 2
    for hr in range(2):
        rsl = slice(hr * half_rows, (hr + 1) * half_rows)
        cs = [c_ref[cc, rsl, :] for cc in range(n_slab)]
        mu = jnp.sum(sum(cs[1:], cs[0]), axis=-1, keepdims=True) * (1.0 / c_conv)
        xcs = [c - mu for c in cs]
        sq = xcs[0] * xcs[0]
        for xc in xcs[1:]:
            sq = sq + xc * xc
        inv = lax.rsqrt(jnp.sum(sq, axis=-1, keepdims=True) * (1.0 / c_conv) + EPS)
        for cc in range(n_slab):
            ch = slice(cc * LANES, (cc + 1) * LANES)
            ln = xcs[cc] * inv * lnw_ref[:, ch] + lnb_ref[:, ch]
            act_ref[rsl, ch] = _silu(ln).astype(BF16)

    cos = cos_ref[0]
    sin = sin_ref[0]
    head_cols = [slice(hh * hd, (hh + 1) * hd) for hh in range(N_RET_HEADS)]
    qs, ks, vbs, scores = [], [], [], []
    for sl in head_cols:
        q = zq[:, sl]
        k = zk[:, sl]
        q = q * cos + pltpu.roll(q, hd // 2, 1) * sin
        k = (k * cos + pltpu.roll(k, hd // 2, 1) * sin) * scale
        qs.append(q)
        ks.append(k)
        vbs.append(zv[:, sl].astype(BF16))
        scores.append(lax.dot_general(q.astype(BF16), k.astype(BF16), (((1,), (1,)), ((), ())),
                                      preferred_element_type=F32))
    inters = []
    for hh in range(N_RET_HEADS):
        qd = (qs[hh] * qdec_ref[hh]).astype(BF16)
        kd = (ks[hh] * kdec_ref[hh]).astype(BF16)
        inter = []
        for s in range(ns):
            rs = slice(s * ts, (s + 1) * ts)
            state = rout_ref[s, hh]
            inter.append(jnp.dot(qd[rs], state.astype(BF16), preferred_element_type=F32))
            kv = lax.dot_general(kd[rs], vbs[hh][rs], (((0,), (0,)), ((), ())),
                                 preferred_element_type=F32)
            rout_ref[s, hh] = state * rdec_ref[hh:hh + 1, :] + kv
        inters.append(inter[0] if ns == 1 else jnp.concatenate(inter, axis=0))
    normed = []
    for hh in range(N_RET_HEADS):
        sc = (scores[hh] * dmat_ref[hh]).astype(BF16)
        o = jnp.dot(sc, vbs[hh], preferred_element_type=F32) + inters[hh]
        mu = jnp.mean(o, axis=-1, keepdims=True)
        oc = o - mu
        on = oc * lax.rsqrt(jnp.mean(oc * oc, axis=-1, keepdims=True) + EPS)
        normed.append(on * gnw_ref[:, head_cols[hh]])

    zg = proj(ret0 + 3 * c_ret, c_ret)
    g_conv = proj(2 * c_conv, c_conv)
    for hh in range(N_RET_HEADS):
        y_ref[:, :, c_conv + hh * hd:c_conv + (hh + 1) * hd] = (
            (_silu(zg[:, hh * hd:(hh + 1) * hd]) * normed[hh]).astype(BF16).reshape(ns, ts, hd))
    cp = jnp.dot(act_ref[...], wpw_ref[...], preferred_element_type=F32)
    y_ref[:, :, 0:c_conv] = (_silu(g_conv) * cp).astype(BF16).reshape(ns, ts, c_conv)


def _mixer_call(xin, npre, layer, weights, tables, bufs_p, r0s, *, ns, ts, vmem_limit):
    win, cw, cb, lnw, lnb, wpw, gnw = weights
    cos_t, sin_t, dmat, qdec, kdec, rdec = tables
    bsz, t_len, d_model = xin.shape
    c_conv = wpw.shape[-1]
    c_ret = gnw.shape[-1]
    hd = c_ret // N_RET_HEADS
    rows = ns * ts
    normalize = npre is not None
    assert ts % CHUNK == 0 and ts % (CONV_ROW_STRIDE * SUBLANES) == 0 and ts >= HIST_ROWS
    assert bsz % ns == 0 and t_len % ts == 0 and c_conv % GLU_CHUNK == 0
    kern = functools.partial(_mixer_kernel, ns=ns, ts=ts, normalize=normalize)
    state_in = pl.BlockSpec((None, ns, N_RET_HEADS, hd, hd), lambda b, t: (layer, b, 0, 0, 0),
                            pipeline_mode=pl.Buffered(1))
    hist_in = pl.BlockSpec((None, ns, HIST_ROWS, c_conv), lambda b, t: (layer, b, 0, 0),
                           pipeline_mode=pl.Buffered(1))
    in_specs = [pl.BlockSpec((ns, ts, d_model), lambda b, t: (b, t, 0))]
    args = [xin]
    if normalize:
        in_specs.append(_layer_resident(npre.shape, layer))
        args.append(npre)
    in_specs += [_layer_resident(w.shape, layer) for w in weights]
    in_specs += [
        pl.BlockSpec((1, rows, hd), lambda b, t: (t, 0, 0)),
        pl.BlockSpec((1, rows, hd), lambda b, t: (t, 0, 0)),
        _resident(dmat.shape),
        _resident(qdec.shape),
        _resident(kdec.shape),
        _resident(rdec.shape),
        hist_in,
        state_in,
    ]
    args += [*weights, cos_t, sin_t, dmat, qdec, kdec, rdec, bufs_p, r0s]
    return pl.pallas_call(
        kern,
        grid=(bsz // ns, t_len // ts),
        in_specs=in_specs,
        out_specs=[
            pl.BlockSpec((ns, ts, c_conv + c_ret), lambda b, t: (b, t, 0)),
            pl.BlockSpec((ns, HIST_ROWS, c_conv), lambda b, t: (b, 0, 0)),
            pl.BlockSpec((ns, N_RET_HEADS, hd, hd), lambda b, t: (b, 0, 0, 0)),
        ],
        out_shape=[
            jax.ShapeDtypeStruct((bsz, t_len, c_conv + c_ret), BF16),
            jax.ShapeDtypeStruct((bsz, HIST_ROWS, c_conv), F32),
            jax.ShapeDtypeStruct((bsz, N_RET_HEADS, hd, hd), F32),
        ],
        scratch_shapes=[
            pltpu.VMEM((ns, c_conv // LANES, HIST_ROWS + ts, LANES), F32),
            pltpu.VMEM((c_conv // LANES, rows, LANES), F32),
            pltpu.VMEM((rows, c_conv), BF16),
        ],
        compiler_params=pltpu.CompilerParams(
            dimension_semantics=("arbitrary", "arbitrary"),
            vmem_limit_bytes=vmem_limit),
        name="mixer",
    )(*args)


def _out_kernel(y_hbm, wout_ref, x_hbm, npost_ref, *refs, emit_next, rows):
    if emit_next:
        nnext_ref, o_hbm, h_hbm = refs
    else:
        (o_hbm,) = refs
    n_tok, d_model = x_hbm.shape

    def tile_body(y_ref, x_ref, o_ref, *maybe_h_ref):
        for p in range(rows // OUT_PART_ROWS):
            rs = slice(p * OUT_PART_ROWS, (p + 1) * OUT_PART_ROWS)
            y = jnp.dot(y_ref[rs, :], wout_ref[...], preferred_element_type=F32)
            x_new = x_ref[rs, :] + _rms_norm(y, npost_ref[...])
            o_ref[rs, :] = x_new
            if emit_next:
                maybe_h_ref[0][rs, :] = _rms_norm(x_new, nnext_ref[...]).astype(BF16)

    def in_spec(width):
        return pl.BlockSpec((rows, width), lambda i: (i, 0), pipeline_mode=pl.Buffered(OUT_IN_BUFFERS))

    row_spec = pl.BlockSpec((rows, d_model), lambda i: (i, 0))
    outs = [o_hbm] + ([h_hbm] if emit_next else [])
    pltpu.emit_pipeline(
        tile_body,
        grid=(n_tok // rows,),
        in_specs=[in_spec(y_hbm.shape[-1]), in_spec(d_model)],
        out_specs=[row_spec] * len(outs),
    )(y_hbm, x_hbm, *outs)


def _out_call(y, wout, x, npost, npre, layer, *, rows):
    n_tok, d_model = x.shape
    emit_next = layer + 1 < npre.shape[0]
    assert n_tok % rows == 0 and rows % OUT_PART_ROWS == 0

    def whole_layer(stacked, l):
        nd = stacked.ndim - 1
        return pl.BlockSpec((None,) + stacked.shape[1:], lambda _: (l,) + (0,) * nd)

    hbm = pl.BlockSpec(memory_space=pl.ANY)
    in_specs = [hbm, whole_layer(wout, layer), hbm, whole_layer(npost, layer)]
    args = [y, wout, x, npost]
    out_specs = [hbm]
    out_shape = [jax.ShapeDtypeStruct((n_tok, d_model), F32)]
    if emit_next:
        in_specs.append(whole_layer(npre, layer + 1))
        args.append(npre)
        out_specs.append(hbm)
        out_shape.append(jax.ShapeDtypeStruct((n_tok, d_model), BF16))
    return pl.pallas_call(
        functools.partial(_out_kernel, emit_next=emit_next, rows=rows),
        grid=(1,),
        in_specs=in_specs,
        out_specs=out_specs,
        out_shape=out_shape,
        compiler_params=pltpu.CompilerParams(vmem_limit_bytes=BIG_VMEM_LIMIT_BYTES),
        name="out_proj",
    )(*args)


def _retention_tables(pos0, t_len, ns, ts, hd):
    n_t = t_len // ts
    half = hd // 2
    inv_freq = ROPE_BASE ** (-jnp.arange(half, dtype=F32) / half)
    pos = pos0 + jnp.arange(t_len)
    ang = pos.astype(F32)[:, None] * inv_freq[None, :]
    cos = jnp.cos(ang)
    sin = jnp.sin(ang)
    cos2 = jnp.concatenate([cos, cos], axis=-1).reshape(n_t, 1, ts, hd)
    sin2 = jnp.concatenate([-sin, sin], axis=-1).reshape(n_t, 1, ts, hd)
    cos_t = jnp.broadcast_to(cos2, (n_t, ns, ts, hd)).reshape(n_t, ns * ts, hd)
    sin_t = jnp.broadcast_to(sin2, (n_t, ns, ts, hd)).reshape(n_t, ns * ts, hd)

    log_g = jnp.log1p(-jnp.exp2(-5.0 - jnp.arange(N_RET_HEADS, dtype=F32)))
    i = jnp.arange(ts, dtype=F32)
    lg = log_g[:, None]
    qdec = jnp.exp((i + 1.0)[None, :] * lg)
    kdec = jnp.exp((ts - 1.0 - i)[None, :] * lg)
    rdec = jnp.exp(ts * log_g)

    def rows_table(v):
        v = jnp.broadcast_to(v[:, None, :, None], (N_RET_HEADS, ns, ts, hd))
        return v.reshape(N_RET_HEADS, ns * ts, hd)

    dist = jnp.abs(i[:, None] - i[None, :])
    dm = jnp.exp(dist[None] * log_g[:, None, None])
    chunk_id = jnp.arange(ts) // CHUNK
    causal = (chunk_id[None, :] <= chunk_id[:, None]).astype(F32)
    dm = dm * causal[None]
    seq_eye = jnp.eye(ns, dtype=F32)
    dmat = (seq_eye[None, :, None, :, None] * dm[:, None, :, None, :]).reshape(
        N_RET_HEADS, ns * ts, ns * ts)
    rdec_t = jnp.broadcast_to(rdec[:, None], (N_RET_HEADS, hd))
    return cos_t, sin_t, dmat, rows_table(qdec), rows_table(kdec), rdec_t


def _trunk(x, pos0, conv_bufs, r0s, params, *, ns, ts, out_rows, vmem_limit):
    norm_pre, mixer_weights, w_out, norm_post = params
    depth = norm_pre.shape[0]
    bsz, t_len, d_model = x.shape
    n_tok = bsz * t_len
    hd = mixer_weights[-1].shape[-1] // N_RET_HEADS
    tables = _retention_tables(pos0, t_len, ns, ts, hd)
    bufs_p = jnp.pad(conv_bufs, ((0, 0), (0, 0), (HIST_PAD, 0), (0, 0)))
    x = x.reshape(n_tok, d_model)
    h = None
    bufs, states = [], []
    for l in range(depth):
        xin = x if h is None else h
        y, nbuf, r_new = _mixer_call(
            xin.reshape(bsz, t_len, d_model), norm_pre if h is None else None, l, mixer_weights, tables,
            bufs_p, r0s, ns=ns, ts=ts, vmem_limit=vmem_limit)
        outs = _out_call(y.reshape(n_tok, -1), w_out, x, norm_post, norm_pre, l, rows=out_rows)
        x = outs[0]
        h = outs[1] if len(outs) > 1 else None
        bufs.append(nbuf[:, HIST_PAD:, :])
        states.append(r_new)
    return x.reshape(bsz, t_len, d_model), jnp.stack(bufs), jnp.stack(states)


def kernel(x_prompt, x_sample, cache_conv, state_ret, norm_pre, w_in, conv_w, conv_b, conv_ln_w,
           conv_ln_b, w_pw, ret_gn_w, w_out, norm_post):
    depth = norm_pre.shape[0]
    bsz, t_len, _ = x_prompt.shape
    c_conv = w_pw.shape[-1]
    hd = ret_gn_w.shape[-1] // N_RET_HEADS
    assert t_len >= CONV_STATE and x_sample.shape[1] >= CONV_STATE

    def rowvec(p):
        return p[:, None, :]

    mixer_weights = (w_in.astype(BF16), conv_w, rowvec(conv_b), rowvec(conv_ln_w), rowvec(conv_ln_b),
                     w_pw.astype(BF16), rowvec(ret_gn_w))
    params = (rowvec(norm_pre), mixer_weights, w_out.astype(BF16), rowvec(norm_post))

    zero_bufs = jnp.zeros((depth, bsz, CONV_STATE, c_conv), F32)
    zero_r = jnp.zeros((depth, bsz, N_RET_HEADS, hd, hd), F32)
    y_p, conv_p, ret_p = _trunk(x_prompt, 0, zero_bufs, zero_r, params, ns=1, ts=min(256, t_len),
                                out_rows=min(512, bsz * t_len), vmem_limit=VMEM_LIMIT_BYTES)
    sb, st, _ = x_sample.shape
    y_s, conv_s, ret_s = _trunk(x_sample, PAST_LEN, cache_conv, state_ret.astype(F32), params,
                                ns=min(4, sb), ts=st, out_rows=min(512, sb * st),
                                vmem_limit=BIG_VMEM_LIMIT_BYTES)
    return (y_p, y_s, conv_p, ret_p, conv_s, ret_s)
```
